```python
import jax, jax.numpy as jnp
from jax import lax
import numpy as np

D_MODEL = 1024
BATCH = 8
SEQ = 4096
DEPTH = 4

GRID_W = 64
CTX_LEN = 256
EPS = 1e-6

N_MIXERS = 3
POOL_SLOT, ATTN_SLOT, SGU_SLOT = 0, 1, 2
N_POOL_LAYERS = len(range(POOL_SLOT, DEPTH, N_MIXERS))
N_ATTN_LAYERS = len(range(ATTN_SLOT, DEPTH, N_MIXERS))
N_SGU_LAYERS = len(range(SGU_SLOT, DEPTH, N_MIXERS))
CTX_LAST_READER = max([i for i in range(DEPTH) if i % N_MIXERS == ATTN_SLOT], default=-1)

POOL_WINDOWS = (2, 4, 8, 16)
POOL_GROUPS = len(POOL_WINDOWS)
POOL_CH = D_MODEL // POOL_GROUPS

HEAD_DIM = 64
N_Q_HEADS = D_MODEL // HEAD_DIM
N_KV_HEADS = 4
Q_PER_KV = N_Q_HEADS // N_KV_HEADS
Q_DIM = N_Q_HEADS * HEAD_DIM
KV_DIM = N_KV_HEADS * HEAD_DIM
QKV_DIM = Q_DIM + 2 * KV_DIM
ROPE_THETA = 10000.0
ROPE_AXIS_DIM = HEAD_DIM // 2
Q_BLOCK = 128

CHUNK = 128
SGU_WIDTH = D_MODEL
SGU_CH = 128
SGU_GROUPS = SGU_WIDTH // SGU_CH

N_EXPERT_GROUPS = 4
EXPERTS_PER_GROUP = 4
N_EXPERTS = N_EXPERT_GROUPS * EXPERTS_PER_GROUP
TOPK_IN_GROUP = 2
D_EXPERT = 256

kernel_name = "hybrid_pool_gqa_gmlp_hmoe_dit"


def rmsnorm(x, g):
    xf = x.astype(jnp.float32)
    y = xf * lax.rsqrt(jnp.mean(xf * xf, axis=-1, keepdims=True) + EPS)
    return (y * g.astype(jnp.float32)).astype(x.dtype)


def layernorm(x, g, b):
    xf = x.astype(jnp.float32)
    mu = jnp.mean(xf, axis=-1, keepdims=True)
    xc = xf - mu
    var = jnp.mean(xc * xc, axis=-1, keepdims=True)
    return (xc * lax.rsqrt(var + EPS) * g.astype(jnp.float32) + b.astype(jnp.float32)).astype(x.dtype)


def adaln(cvec, w_mod, b_mod):
    m = jax.nn.silu(cvec) @ w_mod + b_mod
    m = m.reshape(cvec.shape[:-1] + (6, 1, D_MODEL))
    return tuple(m[..., k, :, :] for k in range(6))


def modulate(x, g, shift, scale):
    return rmsnorm(x, g) * (1.0 + scale) + shift


def rope_tables(rows):
    row = jnp.repeat(jnp.arange(rows), GRID_W)
    col = jnp.tile(jnp.arange(GRID_W), rows)
    inv = ROPE_THETA ** (-jnp.arange(0, ROPE_AXIS_DIM, 2, dtype=jnp.float32) / ROPE_AXIS_DIM)
    ang = jnp.stack([row, col], axis=-1).astype(jnp.float32)[..., None] * inv
    return jnp.cos(ang), jnp.sin(ang)


def apply_rope(x, cos, sin):
    B, n, H, _ = x.shape
    xr = x.astype(jnp.float32).reshape(B, n, H, 2, 2, ROPE_AXIS_DIM // 2)
    x1, x2 = xr[..., 0, :], xr[..., 1, :]
    c, s = cos[:, None], sin[:, None]
    out = jnp.stack([x1 * c - x2 * s, x2 * c + x1 * s], axis=-2)
    return out.reshape(x.shape).astype(x.dtype)


def pool_mixer(h, w_pool, b_pool, ls):
    B, n, _ = h.shape
    t = jnp.arange(n)
    hf = h.astype(jnp.float32)
    outs = []
    for g, w in enumerate(POOL_WINDOWS):
        xg = hf[..., g * POOL_CH:(g + 1) * POOL_CH]
        cs = jnp.concatenate([jnp.zeros_like(xg[:, :1]), jnp.cumsum(xg, axis=1)], axis=1)
        lo = jnp.clip(t - w // 2, 0, n)
        hi = jnp.clip(t + w // 2, 0, n)
        mean = (cs[:, hi] - cs[:, lo]) / (hi - lo).astype(jnp.float32)[:, None]
        outs.append(mean - xg)
    p = jnp.stack(outs, axis=2).astype(h.dtype)
    y = jnp.einsum('bngc,gcd->bngd', p, w_pool) + b_pool
    return y.reshape(h.shape) * ls


def attn_mixer(h_lat, h_ctx, w_qkv, q_g, k_g, w_o, cos, sin, ctx_out):
    B, S, _ = h_lat.shape
    L = h_ctx.shape[1]
    scale = HEAD_DIM ** -0.5
    qkv = h_lat @ w_qkv
    q = rmsnorm(qkv[..., :Q_DIM].reshape(B, S, N_Q_HEADS, HEAD_DIM), q_g)
    k = rmsnorm(qkv[..., Q_DIM:Q_DIM + KV_DIM].reshape(B, S, N_KV_HEADS, HEAD_DIM), k_g)
    v = qkv[..., Q_DIM + KV_DIM:].reshape(B, S, N_KV_HEADS, HEAD_DIM)
    q = apply_rope(q, cos, sin)
    k = apply_rope(k, cos, sin)
    kv_c = h_ctx @ w_qkv[:, Q_DIM:]
    k_c = rmsnorm(kv_c[..., :KV_DIM].reshape(B, L, N_KV_HEADS, HEAD_DIM), k_g)
    v_c = kv_c[..., KV_DIM:].reshape(B, L, N_KV_HEADS, HEAD_DIM)
    k_all = jnp.concatenate([k_c, k], axis=1)
    v_all = jnp.concatenate([v_c, v], axis=1)
    qb = (q * scale).reshape(B, S // Q_BLOCK, Q_BLOCK, N_KV_HEADS, Q_PER_KV, HEAD_DIM)
    qb = jnp.moveaxis(qb, 1, 0)

    def block(q_blk):
        s = jnp.einsum('bqhgd,bkhd->bhgqk', q_blk, k_all).astype(jnp.float32)
        p = jax.nn.softmax(s, axis=-1).astype(v_all.dtype)
        return jnp.einsum('bhgqk,bkhd->bqhgd', p, v_all)

    o = jnp.moveaxis(lax.map(block, qb), 0, 1).reshape(B, S, Q_DIM)
    y_lat = o @ w_o
    y_ctx = None
    if ctx_out:
        q_c = rmsnorm((h_ctx @ w_qkv[:, :Q_DIM]).reshape(B, L, N_Q_HEADS, HEAD_DIM), q_g)
        q_c = (q_c * scale).reshape(B, L, N_KV_HEADS, Q_PER_KV, HEAD_DIM)
        s_c = jnp.einsum('bqhgd,bkhd->bhgqk', q_c, k_c).astype(jnp.float32)
        p_c = jax.nn.softmax(s_c, axis=-1).astype(v_c.dtype)
        y_ctx = jnp.einsum('bhgqk,bkhd->bqhgd', p_c, v_c).reshape(B, L, Q_DIM) @ w_o
    return y_lat, y_ctx


def sgu_mixer(h, w_in, b_in, ln_g, ln_b, w_s, b_s, w_out):
    B, n, _ = h.shape
    z = jax.nn.gelu(h @ w_in + b_in)
    u, v = z[..., :SGU_WIDTH], z[..., SGU_WIDTH:]
    v = layernorm(v, ln_g, ln_b)
    vc = v.reshape(B, n // CHUNK, CHUNK, SGU_GROUPS, SGU_CH)
    sv = jnp.einsum('gpq,bcqgk->bcpgk', w_s, vc) + b_s.T[:, :, None]
    return (u * sv.reshape(B, n, SGU_WIDTH)) @ w_out


def hier_moe(h, w_rg, b_rg, w_re, b_re, w_gate, w_up, w_down):
    shp = h.shape
    hf = h.reshape(-1, D_MODEL)
    n = hf.shape[0]
    g_logits = (hf @ w_rg).astype(jnp.float32) + b_rg
    g_prob = jax.nn.softmax(g_logits, axis=-1)
    g_sel = jnp.argmax(g_logits, axis=-1)
    g_w = jnp.take_along_axis(g_prob, g_sel[:, None], axis=-1)
    e_logits = ((hf @ w_re).astype(jnp.float32) + b_re).reshape(n, N_EXPERT_GROUPS, EXPERTS_PER_GROUP)
    e_in = jnp.take_along_axis(e_logits, g_sel[:, None, None], axis=1)[:, 0]
    top_v, top_i = lax.top_k(e_in, TOPK_IN_GROUP)
    e_w = jax.nn.softmax(top_v, axis=-1) * g_w
    expert_id = g_sel[:, None] * EXPERTS_PER_GROUP + top_i
    gates = jnp.sum(jax.nn.one_hot(expert_id, N_EXPERTS, dtype=jnp.float32) * e_w[..., None], axis=1)
    a = jnp.einsum('nd,edf->nef', hf, w_gate)
    b = jnp.einsum('nd,edf->nef', hf, w_up)
    hid = jax.nn.silu(a) * b * gates.astype(h.dtype)[..., None]
    return jnp.einsum('nef,efd->nd', hid, w_down).reshape(shp)


def setup_inputs(seed: int = 0) -> dict:
    key = jax.random.key(seed)
    ks = iter(jax.random.split(key, 40))
    D = D_MODEL

    def nrm(shape, scale):
        return jax.random.normal(next(ks), shape, jnp.float32) * scale

    return {
        "x": nrm((BATCH, SEQ, D), 1.0),
        "c": nrm((BATCH, D), 1.0),
        "ctx": nrm((BATCH, CTX_LEN, D), 1.0),
        "c_ctx": nrm((D,), 1.0),
        "w_mod": nrm((DEPTH, D, 6 * D), 0.5 * D ** -0.5),
        "b_mod": nrm((DEPTH, 6 * D), 0.02),
        "norm1_g": 1.0 + nrm((DEPTH, D), 0.02),
        "norm2_g": 1.0 + nrm((DEPTH, D), 0.02),
        "pool_w": nrm((N_POOL_LAYERS, POOL_GROUPS, POOL_CH, POOL_CH), POOL_CH ** -0.5),
        "pool_b": nrm((N_POOL_LAYERS, POOL_GROUPS, POOL_CH), 0.02),
        "pool_ls": 1.0 + nrm((N_POOL_LAYERS, D), 0.02),
        "attn_wqkv": nrm((N_ATTN_LAYERS, D, QKV_DIM), D ** -0.5),
        "attn_qg": 1.0 + nrm((N_ATTN_LAYERS, HEAD_DIM), 0.02),
        "attn_kg": 1.0 + nrm((N_ATTN_LAYERS, HEAD_DIM), 0.02),
        "attn_wo": nrm((N_ATTN_LAYERS, Q_DIM, D), Q_DIM ** -0.5),
        "sgu_win": nrm((N_SGU_LAYERS, D, 2 * SGU_WIDTH), D ** -0.5),
        "sgu_bin": nrm((N_SGU_LAYERS, 2 * SGU_WIDTH), 0.02),
        "sgu_lng": 1.0 + nrm((N_SGU_LAYERS, SGU_WIDTH), 0.02),
        "sgu_lnb": nrm((N_SGU_LAYERS, SGU_WIDTH), 0.02),
        "sgu_ws": nrm((N_SGU_LAYERS, SGU_GROUPS, CHUNK, CHUNK), CHUNK ** -0.5),
        "sgu_bs": 1.0 + nrm((N_SGU_LAYERS, SGU_GROUPS, CHUNK), 0.02),
        "sgu_wout": nrm((N_SGU_LAYERS, SGU_WIDTH, D), SGU_WIDTH ** -0.5),
        "moe_wrg": nrm((DEPTH, D, N_EXPERT_GROUPS), D ** -0.5),
        "moe_brg": nrm((DEPTH, N_EXPERT_GROUPS), 0.01),
        "moe_wre": nrm((DEPTH, D, N_EXPERTS), D ** -0.5),
        "moe_bre": nrm((DEPTH, N_EXPERTS), 0.01),
        "moe_wg": nrm((DEPTH, N_EXPERTS, D, D_EXPERT), D ** -0.5),
        "moe_wu": nrm((DEPTH, N_EXPERTS, D, D_EXPERT), D ** -0.5),
        "moe_wd": nrm((DEPTH, N_EXPERTS, D_EXPERT, D), D_EXPERT ** -0.5),
    }


def reference(x, c, ctx, c_ctx, w_mod, b_mod, norm1_g, norm2_g, pool_w, pool_b, pool_ls,
              attn_wqkv, attn_qg, attn_kg, attn_wo, sgu_win, sgu_bin, sgu_lng, sgu_lnb, sgu_ws, sgu_bs,
              sgu_wout, moe_wrg, moe_brg, moe_wre, moe_bre, moe_wg, moe_wu, moe_wd):
    rows = x.shape[1] // GRID_W
    cos, sin = rope_tables(rows)
    s = ctx
    for i in range(DEPTH):
        kind = i % N_MIXERS
        j = i // N_MIXERS
        ctx_in = i <= CTX_LAST_READER
        ctx_upd = i < CTX_LAST_READER
        sh1, sc1, g1, sh2, sc2, g2 = adaln(c, w_mod[i], b_mod[i])
        a_lat = modulate(x, norm1_g[i], sh1, sc1)
        if ctx_in:
            csh1, csc1, cg1, csh2, csc2, cg2 = adaln(c_ctx, w_mod[i], b_mod[i])
            a_ctx = modulate(s, norm1_g[i], csh1, csc1)
        if kind == POOL_SLOT:
            y_lat = pool_mixer(a_lat, pool_w[j], pool_b[j], pool_ls[j])
            if ctx_upd:
                y_ctx = pool_mixer(a_ctx, pool_w[j], pool_b[j], pool_ls[j])
        elif kind == ATTN_SLOT:
            y_lat, y_ctx = attn_mixer(a_lat, a_ctx, attn_wqkv[j], attn_qg[j], attn_kg[j], attn_wo[j],
                                      cos, sin, ctx_upd)
        else:
            y_lat = sgu_mixer(a_lat, sgu_win[j], sgu_bin[j], sgu_lng[j], sgu_lnb[j], sgu_ws[j], sgu_bs[j],
                              sgu_wout[j])
            if ctx_upd:
                y_ctx = sgu_mixer(a_ctx, sgu_win[j], sgu_bin[j], sgu_lng[j], sgu_lnb[j], sgu_ws[j],
                                  sgu_bs[j], sgu_wout[j])
        x = x + g1 * y_lat
        x = x + g2 * hier_moe(modulate(x, norm2_g[i], sh2, sc2), moe_wrg[i], moe_brg[i], moe_wre[i],
                              moe_bre[i], moe_wg[i], moe_wu[i], moe_wd[i])
        if ctx_upd:
            s = s + cg1 * y_ctx
            s = s + cg2 * hier_moe(modulate(s, norm2_g[i], csh2, csc2), moe_wrg[i], moe_brg[i],
                                   moe_wre[i], moe_bre[i], moe_wg[i], moe_wu[i], moe_wd[i])
    return x
```

```python
import functools

import jax
import jax.numpy as jnp
from jax import lax
from jax.experimental import pallas as pl
from jax.experimental.pallas import tpu as pltpu

F32 = jnp.float32
BF16 = jnp.bfloat16

D = 1024
GRID_W = 64
EPS = 1e-6
N_MIX = 3
POOL_WINDOWS = (2, 4, 8, 16)
POOL_CH = D // len(POOL_WINDOWS)
POOL_HALO = 8
HEAD_DIM = 64
N_KV = 4
Q_PER_KV = 4
KV_DIM = N_KV * HEAD_DIM
ROPE_THETA = 10000.0
CHUNK = 128
SGU_G = 8
N_GROUPS = 4
EPG = 4
D_EXPERT = 256
N_MOD = 6
LANES = 128
CTX_ROW = 8
MOD_ROWS = 16
VMEM_LIMIT = 56 * 1024 * 1024


def _dot(a, b):
    return jnp.dot(a, b, preferred_element_type=F32)


def _rms_mod(x, g, sh, sc):
    ms = jnp.mean(x * x, axis=-1, keepdims=True)
    return x * lax.rsqrt(ms + EPS) * g * (1.0 + sc) + sh


def _silu(x):
    return x * (1.0 / (1.0 + jnp.exp(-x)))


def _route(h, wrh_ref, wrl_ref, br_ref):
    h_hi = h.astype(BF16)
    h_lo = (h - h_hi.astype(F32)).astype(BF16)
    whi = wrh_ref[...]
    lg = _dot(h_hi, whi) + _dot(h_hi, wrl_ref[...]) + _dot(h_lo, whi) + br_ref[...]
    lane = lax.broadcasted_iota(jnp.int32, lg.shape, 1)
    lane_f = lane.astype(F32)
    neg = -jnp.inf
    gmask = lane < N_GROUPS
    gl = jnp.where(gmask, lg, neg)
    gmax = jnp.max(gl, axis=-1, keepdims=True)
    gsel = jnp.min(jnp.where(gl == gmax, lane_f, float(LANES)), axis=-1, keepdims=True)
    gsum = jnp.sum(jnp.exp(gl - gmax), axis=-1, keepdims=True)
    g_w = 1.0 / gsum
    lo_lane = N_GROUPS + EPG * gsel
    emask = (lane_f >= lo_lane) & (lane_f < lo_lane + EPG)
    el = jnp.where(emask, lg, neg)
    t1 = jnp.max(el, axis=-1, keepdims=True)
    i1 = jnp.min(jnp.where(el == t1, lane_f, float(LANES)), axis=-1, keepdims=True)
    el2 = jnp.where(lane_f == i1, neg, el)
    t2 = jnp.max(el2, axis=-1, keepdims=True)
    i2 = jnp.min(jnp.where(el2 == t2, lane_f, float(LANES)), axis=-1, keepdims=True)
    e = jnp.exp(t2 - t1)
    w1 = 1.0 / (1.0 + e)
    w2 = e / (1.0 + e)
    return jnp.where(lane_f == i1, w1 * g_w, jnp.where(lane_f == i2, w2 * g_w, 0.0))


def _post_mixer(x, y, g1, n2g, sh2, sc2, wrh_ref, wrl_ref, br_ref, x1_ref, h2_ref, gates_ref):
    x1 = x + g1 * y
    h2 = _rms_mod(x1, n2g, sh2, sc2)
    x1_ref[...] = x1
    h2_ref[...] = h2.astype(BF16)
    gates_ref[...] = _route(h2, wrh_ref, wrl_ref, br_ref)


def _adaln_kernel(cc_ref, w_ref, b_ref, o_ref):
    s = _silu(cc_ref[...]).astype(BF16)
    o_ref[...] = _dot(s, w_ref[...].astype(BF16)) + b_ref[...]


def _adaln(cc, w_mod, b_mod):
    depth = w_mod.shape[0]
    return pl.pallas_call(
        _adaln_kernel,
        out_shape=jax.ShapeDtypeStruct((depth, MOD_ROWS, N_MOD * D), F32),
        grid=(depth, N_MOD),
        in_specs=[
            pl.BlockSpec((MOD_ROWS, D), lambda i, k: (0, 0)),
            pl.BlockSpec((None, D, D), lambda i, k: (i, 0, k)),
            pl.BlockSpec((None, 1, D), lambda i, k: (i, 0, k)),
        ],
        out_specs=pl.BlockSpec((None, MOD_ROWS, D), lambda i, k: (i, 0, k)),
        compiler_params=pltpu.CompilerParams(dimension_semantics=("arbitrary", "arbitrary")),
        name="adaln",
    )(cc, w_mod, b_mod.reshape(depth, 1, N_MOD * D))


def _mod_spec(k, row_fn):
    return pl.BlockSpec((None, None, 1, D), lambda *idx: (row_fn(*idx), k, 0, 0))


def _full_spec(shape):
    nd = len(shape)
    return pl.BlockSpec(shape, lambda *idx: (0,) * nd)


def _router_specs():
    return [_full_spec((D, LANES)), _full_spec((D, LANES)), _full_spec((1, LANES))]


def _post_out(rows_shape, ts):
    b, s = rows_shape
    shapes = (
        jax.ShapeDtypeStruct((b, s, D), F32),
        jax.ShapeDtypeStruct((b, s, D), BF16),
        jax.ShapeDtypeStruct((b, s, LANES), F32),
    )
    specs = (
        pl.BlockSpec((None, ts, D), lambda bi, i: (bi, i, 0)),
        pl.BlockSpec((None, ts, D), lambda bi, i: (bi, i, 0)),
        pl.BlockSpec((None, ts, LANES), lambda bi, i: (bi, i, 0)),
    )
    return shapes, specs


def _pool_kernel(xp_ref, xc_ref, xn_ref, sh1, sc1, g1, sh2, sc2, n1g, n2g, pw_ref, pb_ref, pls_ref,
                 wrh_ref, wrl_ref, br_ref, x1_ref, h2_ref, gates_ref, *, seq, ts):
    i = pl.program_id(1)
    nt = pl.num_programs(1)
    x = xc_ref[...]
    g, sh, sc = n1g[...], sh1[...], sc1[...]
    a_c = _rms_mod(x, g, sh, sc)
    a_p = _rms_mod(xp_ref[...], g, sh, sc) * (i > 0).astype(F32)
    a_n = _rms_mod(xn_ref[...], g, sh, sc) * (i < nt - 1).astype(F32)
    a_ext = jnp.concatenate([a_p, a_c, a_n], axis=0)
    n_ext = ts + 2 * POOL_HALO
    t = i * ts + lax.broadcasted_iota(jnp.int32, (ts, 1), 0)
    ys = []
    for gi, w in enumerate(POOL_WINDOWS):
        ag = a_ext[:, gi * POOL_CH:(gi + 1) * POOL_CH]
        acc = ag + pltpu.roll(ag, 1, 0)
        half = 1
        while 2 * half < w:
            acc = pltpu.roll(acc, half, 0) + pltpu.roll(acc, n_ext - half, 0)
            half *= 2
        win = acc[POOL_HALO:POOL_HALO + ts]
        cnt = (jnp.minimum(t + w // 2, seq) - jnp.maximum(t - w // 2, 0)).astype(F32)
        p = win / cnt - ag[POOL_HALO:POOL_HALO + ts]
        ys.append(_dot(p.astype(BF16), pw_ref[gi]))
    y = (jnp.concatenate(ys, axis=1) + pb_ref[...]) * pls_ref[...]
    _post_mixer(x, y, g1[...], n2g[...], sh2[...], sc2[...], wrh_ref, wrl_ref, br_ref,
                x1_ref, h2_ref, gates_ref)


def _pool_layer(x, m, row_fn, n1g, n2g, pw, pb, pls, router):
    b, s, _ = x.shape
    ts = min(512, s)
    nh = s // POOL_HALO
    r = ts // POOL_HALO
    shapes, specs = _post_out((b, s), ts)
    rf = lambda bi, i: row_fn(bi)
    return pl.pallas_call(
        functools.partial(_pool_kernel, seq=s, ts=ts),
        out_shape=shapes,
        grid=(b, s // ts),
        in_specs=[
            pl.BlockSpec((None, POOL_HALO, D), lambda bi, i: (bi, jnp.maximum(i * r - 1, 0), 0)),
            pl.BlockSpec((None, ts, D), lambda bi, i: (bi, i, 0)),
            pl.BlockSpec((None, POOL_HALO, D), lambda bi, i: (bi, jnp.minimum((i + 1) * r, nh - 1), 0)),
            _mod_spec(0, rf), _mod_spec(1, rf), _mod_spec(2, rf), _mod_spec(3, rf), _mod_spec(4, rf),
            _full_spec((1, D)), _full_spec((1, D)),
            _full_spec(pw.shape), _full_spec((1, D)), _full_spec((1, D)),
        ] + _router_specs(),
        out_specs=specs,
        compiler_params=pltpu.CompilerParams(dimension_semantics=("arbitrary", "arbitrary"),
                                             vmem_limit_bytes=VMEM_LIMIT),
        name="pool_layer",
    )(x, x, x, m, m, m, m, m, n1g, n2g, pw, pb, pls, *router)


def _moe_kernel(h_ref, x1_ref, gates_ref, g2_ref, wg_ref, wu_ref, wd_ref, o_ref, acc_ref):
    g = pl.program_id(1)
    h = h_ref[...]
    gates = gates_ref[...]
    lane = lax.broadcasted_iota(jnp.int32, gates.shape, 1)
    parts = []
    for e in range(EPG):
        a = _dot(h, wg_ref[e])
        u = _dot(h, wu_ref[e])
        ge = jnp.sum(jnp.where(lane == N_GROUPS + EPG * g + e, gates, 0.0), axis=-1, keepdims=True)
        parts.append((_silu(a) * u * ge).astype(BF16))
    hid = jnp.concatenate(parts, axis=1)
    y = _dot(hid, wd_ref[...].reshape(EPG * D_EXPERT, D))

    @pl.when(g == 0)
    def _():
        acc_ref[...] = y

    @pl.when(g > 0)
    def _():
        acc_ref[...] += y

    @pl.when(g == N_GROUPS - 1)
    def _():
        o_ref[...] = x1_ref[...] + g2_ref[...] * acc_ref[...]


def _moe_layer(h2, x1, gates, m, row_fn, wg, wu, wd):
    b, s, _ = x1.shape
    n = b * s
    tm = min(1024, s)
    per_b = s // tm
    h2 = h2.reshape(n, D)
    x1 = x1.reshape(n, D)
    gates = gates.reshape(n, LANES)
    out = pl.pallas_call(
        _moe_kernel,
        out_shape=jax.ShapeDtypeStruct((n, D), F32),
        grid=(n // tm, N_GROUPS),
        in_specs=[
            pl.BlockSpec((tm, D), lambda t, g: (t, 0)),
            pl.BlockSpec((tm, D), lambda t, g: (t, 0)),
            pl.BlockSpec((tm, LANES), lambda t, g: (t, 0)),
            _mod_spec(5, lambda t, g: row_fn(t // per_b)),
            pl.BlockSpec((EPG, D, D_EXPERT), lambda t, g: (g, 0, 0)),
            pl.BlockSpec((EPG, D, D_EXPERT), lambda t, g: (g, 0, 0)),
            pl.BlockSpec((EPG, D_EXPERT, D), lambda t, g: (g, 0, 0)),
        ],
        out_specs=pl.BlockSpec((tm, D), lambda t, g: (t, 0)),
        scratch_shapes=[pltpu.VMEM((tm, D), F32)],
        compiler_params=pltpu.CompilerParams(dimension_semantics=("arbitrary", "arbitrary"),
                                             vmem_limit_bytes=VMEM_LIMIT),
        name="moe_layer",
    )(h2, x1, gates, m, wg, wu, wd)
    return out.reshape(b, s, D)


def _qkv_kernel(x_ref, sh1, sc1, n1g, w_ref, qg_ref, kg_ref, cos_ref, sin_ref, *out_refs, with_q):
    a = _rms_mod(x_ref[...], n1g[...], sh1[...], sc1[...]).astype(BF16)
    qkv = _dot(a, w_ref[...])
    cos, sin = cos_ref[...], sin_ref[...]
    lane = lax.broadcasted_iota(jnp.int32, cos.shape, 1)
    first_head = lane < HEAD_DIM
    first_half = (lane & 31) < 16

    def norm_rope(t, g):
        sq = t * t
        s_all = jnp.sum(sq, axis=-1, keepdims=True)
        s_lo = jnp.sum(jnp.where(first_head, sq, 0.0), axis=-1, keepdims=True)
        ms = jnp.where(first_head, s_lo, s_all - s_lo) * (1.0 / HEAD_DIM)
        tn = t * lax.rsqrt(ms + EPS) * g
        up = pltpu.roll(tn, LANES - 16, 1)
        dn = pltpu.roll(tn, 16, 1)
        return tn * cos + jnp.where(first_half, up, dn) * sin

    if with_q:
        q_ref, k_ref, v_ref = out_refs
        qg = qg_ref[...]
        for j in range(D // LANES):
            blk = norm_rope(qkv[:, j * LANES:(j + 1) * LANES], qg) * (HEAD_DIM ** -0.5)
            q_ref[:, j * LANES:(j + 1) * LANES] = blk.astype(BF16)
        off = D
    else:
        k_ref, v_ref = out_refs
        off = 0
    kg = kg_ref[...]
    for j in range(KV_DIM // LANES):
        kn = norm_rope(qkv[:, off + j * LANES:off + (j + 1) * LANES], kg).astype(BF16)
        k_ref[2 * j] = kn[:, :HEAD_DIM]
        k_ref[2 * j + 1] = kn[:, HEAD_DIM:]
        vv = qkv[:, off + KV_DIM + j * LANES:off + KV_DIM + (j + 1) * LANES].astype(BF16)
        v_ref[2 * j] = vv[:, :HEAD_DIM]
        v_ref[2 * j + 1] = vv[:, HEAD_DIM:]


def _qkv_layer(x, m, row_fn, n1g, w, qg, kg, cos_t, sin_t, with_q):
    b, s, _ = x.shape
    ts = min(512, s)
    rf = lambda bi, i: row_fn(bi)
    kv_shape = jax.ShapeDtypeStruct((b, N_KV, s, HEAD_DIM), BF16)
    kv_spec = pl.BlockSpec((None, N_KV, ts, HEAD_DIM), lambda bi, i: (bi, 0, i, 0))
    shapes, specs = [kv_shape, kv_shape], [kv_spec, kv_spec]
    if with_q:
        shapes = [jax.ShapeDtypeStruct((b, s, D), BF16)] + shapes
        specs = [pl.BlockSpec((None, ts, D), lambda bi, i: (bi, i, 0))] + specs
    return pl.pallas_call(
        functools.partial(_qkv_kernel, with_q=with_q),
        out_shape=tuple(shapes),
        grid=(b, s // ts),
        in_specs=[
            pl.BlockSpec((None, ts, D), lambda bi, i: (bi, i, 0)),
            _mod_spec(0, rf), _mod_spec(1, rf),
            _full_spec((1, D)),
            _full_spec(w.shape),
            _full_spec((1, LANES)), _full_spec((1, LANES)),
            pl.BlockSpec((ts, LANES), lambda bi, i: (i, 0)),
            pl.BlockSpec((ts, LANES), lambda bi, i: (i, 0)),
        ],
        out_specs=tuple(specs),
        compiler_params=pltpu.CompilerParams(dimension_semantics=("arbitrary", "arbitrary"),
                                             vmem_limit_bytes=VMEM_LIMIT),
        name="qkv_layer",
    )(x, m, m, n1g, w, qg, kg, cos_t, sin_t)


def _attn_kernel(q_ref, k_ref, v_ref, o_ref, *, tq):
    q = q_ref[...]
    q4 = jnp.concatenate([q[:, j * HEAD_DIM:(j + 1) * HEAD_DIM] for j in range(Q_PER_KV)], axis=0)
    s = lax.dot_general(q4, k_ref[...], (((1,), (1,)), ((), ())), preferred_element_type=F32)
    m = jnp.max(s, axis=-1, keepdims=True)
    p = jnp.exp(s - m)
    l = jnp.sum(p, axis=-1, keepdims=True)
    o = _dot(p.astype(BF16), v_ref[...]) / l
    for j in range(Q_PER_KV):
        o_ref[:, j * HEAD_DIM:(j + 1) * HEAD_DIM] = o[j * tq:(j + 1) * tq].astype(BF16)


def _attention(q, k_all, v_all):
    b, s, _ = q.shape
    nk = k_all.shape[2]
    tq = 128
    grp = Q_PER_KV * HEAD_DIM
    return pl.pallas_call(
        functools.partial(_attn_kernel, tq=tq),
        out_shape=jax.ShapeDtypeStruct((b, s, D), BF16),
        grid=(b, N_KV, s // tq),
        in_specs=[
            pl.BlockSpec((None, tq, grp), lambda bi, h, i: (bi, i, h)),
            pl.BlockSpec((None, None, nk, HEAD_DIM), lambda bi, h, i: (bi, h, 0, 0)),
            pl.BlockSpec((None, None, nk, HEAD_DIM), lambda bi, h, i: (bi, h, 0, 0)),
        ],
        out_specs=pl.BlockSpec((None, tq, grp), lambda bi, h, i: (bi, i, h)),
        compiler_params=pltpu.CompilerParams(dimension_semantics=("arbitrary", "arbitrary", "arbitrary"),
                                             vmem_limit_bytes=VMEM_LIMIT),
        name="attention",
    )(q, k_all, v_all)


def _proj_kernel(o_ref, x_ref, g1, sh2, sc2, n2g, wo_ref, wrh_ref, wrl_ref, br_ref,
                 x1_ref, h2_ref, gates_ref):
    y = _dot(o_ref[...], wo_ref[...])
    _post_mixer(x_ref[...], y, g1[...], n2g[...], sh2[...], sc2[...], wrh_ref, wrl_ref, br_ref,
                x1_ref, h2_ref, gates_ref)


def _proj_layer(o, x, m, row_fn, n2g, wo, router):
    b, s, _ = x.shape
    ts = min(512, s)
    rf = lambda bi, i: row_fn(bi)
    shapes, specs = _post_out((b, s), ts)
    return pl.pallas_call(
        _proj_kernel,
        out_shape=shapes,
        grid=(b, s // ts),
        in_specs=[
            pl.BlockSpec((None, ts, D), lambda bi, i: (bi, i, 0)),
            pl.BlockSpec((None, ts, D), lambda bi, i: (bi, i, 0)),
            _mod_spec(2, rf), _mod_spec(3, rf), _mod_spec(4, rf),
            _full_spec((1, D)),
            _full_spec((D, D)),
        ] + _router_specs(),
        out_specs=specs,
        compiler_params=pltpu.CompilerParams(dimension_semantics=("arbitrary", "arbitrary"),
                                             vmem_limit_bytes=VMEM_LIMIT),
        name="attn_proj",
    )(o, x, m, m, m, n2g, wo, *router)


def _gelu_tanh(x):
    c = 0.7978845608028654
    return x * (0.5 * (1.0 + jnp.tanh(c * (x + 0.044715 * (x * x * x)))))


def _sgu_kernel(x_ref, sh1, sc1, g1, sh2, sc2, n1g, n2g, win_ref, bin_ref, lng_ref, lnb_ref, ws_ref,
                bs_ref, wout_ref, wrh_ref, wrl_ref, br_ref, x1_ref, h2_ref, gates_ref, *, ts):
    x = x_ref[...]
    a = _rms_mod(x, n1g[...], sh1[...], sc1[...]).astype(BF16)
    z = _gelu_tanh(_dot(a, win_ref[...]) + bin_ref[...])
    u, v = z[:, :D], z[:, D:]
    mu = jnp.mean(v, axis=-1, keepdims=True)
    vc = v - mu
    var = jnp.mean(vc * vc, axis=-1, keepdims=True)
    vb = (vc * lax.rsqrt(var + EPS) * lng_ref[...] + lnb_ref[...]).astype(BF16)
    bias = bs_ref[...]
    rows = []
    for c in range(ts // CHUNK):
        cols = [_dot(ws_ref[g], vb[c * CHUNK:(c + 1) * CHUNK, g * LANES:(g + 1) * LANES])
                for g in range(SGU_G)]
        rows.append(jnp.concatenate(cols, axis=1) + bias)
    sv = jnp.concatenate(rows, axis=0)
    y = _dot((u * sv).astype(BF16), wout_ref[...])
    _post_mixer(x, y, g1[...], n2g[...], sh2[...], sc2[...], wrh_ref, wrl_ref, br_ref,
                x1_ref, h2_ref, gates_ref)


def _sgu_layer(x, m, row_fn, n1g, n2g, win, b_in, lng, lnb, ws, bs_full, wout, router):
    b, s, _ = x.shape
    ts = min(512, s)
    rf = lambda bi, i: row_fn(bi)
    shapes, specs = _post_out((b, s), ts)
    return pl.pallas_call(
        functools.partial(_sgu_kernel, ts=ts),
        out_shape=shapes,
        grid=(b, s // ts),
        in_specs=[
            pl.BlockSpec((None, ts, D), lambda bi, i: (bi, i, 0)),
            _mod_spec(0, rf), _mod_spec(1, rf), _mod_spec(2, rf), _mod_spec(3, rf), _mod_spec(4, rf),
            _full_spec((1, D)), _full_spec((1, D)),
            _full_spec((D, 2 * D)), _full_spec((1, 2 * D)),
            _full_spec((1, D)), _full_spec((1, D)),
            _full_spec((SGU_G, CHUNK, CHUNK)), _full_spec((CHUNK, D)),
            _full_spec((D, D)),
        ] + _router_specs(),
        out_specs=specs,
        compiler_params=pltpu.CompilerParams(dimension_semantics=("arbitrary", "arbitrary"),
                                             vmem_limit_bytes=VMEM_LIMIT),
        name="sgu_layer",
    )(x, m, m, m, m, m, n1g, n2g, win, b_in, lng, lnb, ws, bs_full, wout, *router)


def _rope_tables(s):
    t = jnp.arange(s)
    pos = jnp.stack([t // GRID_W, t % GRID_W], axis=-1).astype(F32)
    inv = ROPE_THETA ** (-jnp.arange(0, HEAD_DIM // 2, 2, dtype=F32) / (HEAD_DIM // 2))
    lane = jnp.arange(LANES)
    d = lane % HEAD_DIM
    ang = pos[:, d // 32] * inv[d % 16][None, :]
    sign = jnp.where((d % 32) < 16, -1.0, 1.0).astype(F32)
    return jnp.cos(ang), jnp.sin(ang) * sign[None, :]


def _router_params(w_rg, b_rg, w_re, b_re):
    w = jnp.concatenate([w_rg, w_re], axis=1)
    w = jnp.pad(w, ((0, 0), (0, LANES - w.shape[1])))
    hi = w.astype(BF16)
    lo = (w - hi.astype(F32)).astype(BF16)
    bias = jnp.concatenate([b_rg, b_re])
    bias = jnp.pad(bias, (0, LANES - bias.shape[0])).reshape(1, LANES)
    return hi, lo, bias


def kernel(x, c, ctx, c_ctx, w_mod, b_mod, norm1_g, norm2_g, pool_w, pool_b, pool_ls, attn_wqkv, attn_qg,
           attn_kg, attn_wo, sgu_win, sgu_bin, sgu_lng, sgu_lnb, sgu_ws, sgu_bs, sgu_wout, moe_wrg, moe_brg,
           moe_wre, moe_bre, moe_wg, moe_wu, moe_wd):
    b, s, _ = x.shape
    depth = w_mod.shape[0]
    assert b <= CTX_ROW and x.shape[2] == D and s % GRID_W == 0
    attn_layers = [i for i in range(depth) if i % N_MIX == 1]
    ctx_last = max(attn_layers, default=-1)

    cc = jnp.zeros((MOD_ROWS, D), F32).at[:b].set(c).at[CTX_ROW].set(c_ctx)
    mods = _adaln(cc, w_mod, b_mod).reshape(depth, MOD_ROWS, N_MOD, 1, D)
    lat_row = lambda bi: bi
    ctx_row = lambda bi: CTX_ROW
    cos_t, sin_t = _rope_tables(s)
    ones_t = jnp.ones((ctx.shape[1], LANES), F32)
    zeros_t = jnp.zeros((ctx.shape[1], LANES), F32)

    sctx = ctx
    for i in range(depth):
        kind, j = i % N_MIX, i // N_MIX
        ctx_in, ctx_upd = i <= ctx_last, i < ctx_last
        m = mods[i]
        n1g, n2g = norm1_g[i].reshape(1, D), norm2_g[i].reshape(1, D)
        router = _router_params(moe_wrg[i], moe_brg[i], moe_wre[i], moe_bre[i])
        wg, wu, wd = moe_wg[i].astype(BF16), moe_wu[i].astype(BF16), moe_wd[i].astype(BF16)
        streams = [(x, lat_row)] + ([(sctx, ctx_row)] if ctx_upd else [])
        outs = []
        if kind == 0:
            pw, pb = pool_w[j].astype(BF16), pool_b[j].reshape(1, D)
            pls = pool_ls[j].reshape(1, D)
            for xs, rf in streams:
                outs.append(_pool_layer(xs, m, rf, n1g, n2g, pw, pb, pls, router))
        elif kind == 1:
            w = attn_wqkv[j].astype(BF16)
            qg = jnp.tile(attn_qg[j], LANES // HEAD_DIM).reshape(1, LANES)
            kg = jnp.tile(attn_kg[j], LANES // HEAD_DIM).reshape(1, LANES)
            q, k, v = _qkv_layer(x, m, lat_row, n1g, w, qg, kg, cos_t, sin_t, True)
            k_c, v_c = _qkv_layer(sctx, m, ctx_row, n1g, w[:, D:], qg, kg, ones_t, zeros_t, False)
            o = _attention(q, jnp.concatenate([k_c, k], axis=2), jnp.concatenate([v_c, v], axis=2))
            outs.append(_proj_layer(o, x, m, lat_row, n2g, attn_wo[j].astype(BF16), router))
            assert not ctx_upd
        else:
            bs_full = jnp.repeat(sgu_bs[j].T, LANES, axis=1)
            args = (sgu_win[j].astype(BF16), sgu_bin[j].reshape(1, 2 * D), sgu_lng[j].reshape(1, D),
                    sgu_lnb[j].reshape(1, D), sgu_ws[j].astype(BF16), bs_full, sgu_wout[j].astype(BF16))
            for xs, rf in streams:
                outs.append(_sgu_layer(xs, m, rf, n1g, n2g, *args, router))
        new = [_moe_layer(h2, x1, gates, m, rf, wg, wu, wd)
               for (x1, h2, gates), (_, rf) in zip(outs, streams)]
        x = new[0]
        if ctx_upd:
            sctx = new[1]
    return x
```

```python
import functools

import jax
import jax.numpy as jnp
from jax import lax
from jax.experimental import pallas as pl
from jax.experimental.pallas import tpu as pltpu

F32 = jnp.float32
BF16 = jnp.bfloat16

D = 1024
GRID_W = 64
EPS = 1e-6
N_MIX = 3
POOL_WINDOWS = (2, 4, 8, 16)
POOL_CH = D // len(POOL_WINDOWS)
POOL_HALO = 8
HEAD_DIM = 64
N_KV = 4
Q_PER_KV = 4
KV_DIM = N_KV * HEAD_DIM
ROPE_THETA = 10000.0
Q_SCALE = HEAD_DIM ** -0.5 * 1.4426950408889634
CHUNK = 128
SGU_G = 8
N_GROUPS = 4
EPG = 4
D_EXPERT = 256
N_MOD = 6
LANES = 128
CTX_ROW = 8
MOD_ROWS = 16
VMEM_LIMIT = 56 * 1024 * 1024


def _dot(a, b):
    return jnp.dot(a, b, preferred_element_type=F32)


def _rms_mod(x, g, sh, sc):
    ms = jnp.mean(x * x, axis=-1, keepdims=True)
    return x * lax.rsqrt(ms + EPS) * g * (1.0 + sc) + sh


def _silu(x):
    return x * (1.0 / (1.0 + jnp.exp(-x)))


def _route(h, wrh_ref, wrl_ref, br_ref):
    h_hi = h.astype(BF16)
    h_lo = (h - h_hi.astype(F32)).astype(BF16)
    whi = wrh_ref[...]
    lg = _dot(h_hi, whi) + _dot(h_hi, wrl_ref[...]) + _dot(h_lo, whi) + br_ref[...]
    lane = lax.broadcasted_iota(jnp.int32, lg.shape, 1)
    lane_f = lane.astype(F32)
    neg = -jnp.inf
    gmask = lane < N_GROUPS
    gl = jnp.where(gmask, lg, neg)
    gmax = jnp.max(gl, axis=-1, keepdims=True)
    gsel = jnp.min(jnp.where(gl == gmax, lane_f, float(LANES)), axis=-1, keepdims=True)
    gsum = jnp.sum(jnp.exp(gl - gmax), axis=-1, keepdims=True)
    g_w = 1.0 / gsum
    lo_lane = N_GROUPS + EPG * gsel
    emask = (lane_f >= lo_lane) & (lane_f < lo_lane + EPG)
    el = jnp.where(emask, lg, neg)
    t1 = jnp.max(el, axis=-1, keepdims=True)
    i1 = jnp.min(jnp.where(el == t1, lane_f, float(LANES)), axis=-1, keepdims=True)
    el2 = jnp.where(lane_f == i1, neg, el)
    t2 = jnp.max(el2, axis=-1, keepdims=True)
    i2 = jnp.min(jnp.where(el2 == t2, lane_f, float(LANES)), axis=-1, keepdims=True)
    e = jnp.exp(t2 - t1)
    w1 = 1.0 / (1.0 + e)
    w2 = e / (1.0 + e)
    return jnp.where(lane_f == i1, w1 * g_w, jnp.where(lane_f == i2, w2 * g_w, 0.0))


def _post_mixer(x, y, g1, n2g, sh2, sc2, wrh_ref, wrl_ref, br_ref, x1_ref, h2_ref, gates_ref):
    x1 = x + g1 * y
    h2 = _rms_mod(x1, n2g, sh2, sc2)
    x1_ref[...] = x1
    h2_ref[...] = h2.astype(BF16)
    gates_ref[...] = _route(h2, wrh_ref, wrl_ref, br_ref)


def _adaln_kernel(cc_ref, w_ref, b_ref, o_ref):
    s = _silu(cc_ref[...]).astype(BF16)
    o_ref[...] = _dot(s, w_ref[...].astype(BF16)) + b_ref[...]


def _adaln(cc, w_mod, b_mod):
    depth = w_mod.shape[0]
    return pl.pallas_call(
        _adaln_kernel,
        out_shape=jax.ShapeDtypeStruct((depth, MOD_ROWS, N_MOD * D), F32),
        grid=(depth, N_MOD),
        in_specs=[
            pl.BlockSpec((MOD_ROWS, D), lambda i, k: (0, 0)),
            pl.BlockSpec((None, D, D), lambda i, k: (i, 0, k)),
            pl.BlockSpec((None, 1, D), lambda i, k: (i, 0, k)),
        ],
        out_specs=pl.BlockSpec((None, MOD_ROWS, D), lambda i, k: (i, 0, k)),
        compiler_params=pltpu.CompilerParams(dimension_semantics=("arbitrary", "arbitrary")),
        name="adaln",
    )(cc, w_mod, b_mod.reshape(depth, 1, N_MOD * D))


def _mod_spec(k, row_fn):
    return pl.BlockSpec((None, None, 1, D), lambda *idx: (row_fn(*idx), k, 0, 0))


def _full_spec(shape):
    nd = len(shape)
    return pl.BlockSpec(shape, lambda *idx: (0,) * nd)


def _router_specs():
    return [_full_spec((D, LANES)), _full_spec((D, LANES)), _full_spec((1, LANES))]


def _post_out(rows_shape, ts):
    b, s = rows_shape
    shapes = (
        jax.ShapeDtypeStruct((b, s, D), F32),
        jax.ShapeDtypeStruct((b, s, D), BF16),
        jax.ShapeDtypeStruct((b, s, LANES), F32),
    )
    specs = (
        pl.BlockSpec((None, ts, D), lambda bi, i: (bi, i, 0)),
        pl.BlockSpec((None, ts, D), lambda bi, i: (bi, i, 0)),
        pl.BlockSpec((None, ts, LANES), lambda bi, i: (bi, i, 0)),
    )
    return shapes, specs


def _pool_kernel(xp_ref, xc_ref, xn_ref, sh1, sc1, g1, sh2, sc2, n1g, n2g, pw_ref, pb_ref, pls_ref,
                 wrh_ref, wrl_ref, br_ref, x1_ref, h2_ref, gates_ref, *, seq, ts):
    i = pl.program_id(1)
    nt = pl.num_programs(1)
    x = xc_ref[...]
    g, sh, sc = n1g[...], sh1[...], sc1[...]
    a_c = _rms_mod(x, g, sh, sc)
    a_p = _rms_mod(xp_ref[...], g, sh, sc) * (i > 0).astype(F32)
    a_n = _rms_mod(xn_ref[...], g, sh, sc) * (i < nt - 1).astype(F32)
    a_ext = jnp.concatenate([a_p, a_c, a_n], axis=0)
    n_ext = ts + 2 * POOL_HALO
    t = i * ts + lax.broadcasted_iota(jnp.int32, (ts, 1), 0)
    ys = []
    for gi, w in enumerate(POOL_WINDOWS):
        ag = a_ext[:, gi * POOL_CH:(gi + 1) * POOL_CH]
        acc = ag + pltpu.roll(ag, 1, 0)
        half = 1
        while 2 * half < w:
            acc = pltpu.roll(acc, half, 0) + pltpu.roll(acc, n_ext - half, 0)
            half *= 2
        win = acc[POOL_HALO:POOL_HALO + ts]
        cnt = (jnp.minimum(t + w // 2, seq) - jnp.maximum(t - w // 2, 0)).astype(F32)
        p = win / cnt - ag[POOL_HALO:POOL_HALO + ts]
        ys.append(_dot(p.astype(BF16), pw_ref[gi]))
    y = (jnp.concatenate(ys, axis=1) + pb_ref[...]) * pls_ref[...]
    _post_mixer(x, y, g1[...], n2g[...], sh2[...], sc2[...], wrh_ref, wrl_ref, br_ref,
                x1_ref, h2_ref, gates_ref)


def _pool_layer(x, m, row_fn, n1g, n2g, pw, pb, pls, router):
    b, s, _ = x.shape
    ts = min(512, s)
    nh = s // POOL_HALO
    r = ts // POOL_HALO
    shapes, specs = _post_out((b, s), ts)
    rf = lambda bi, i: row_fn(bi)
    return pl.pallas_call(
        functools.partial(_pool_kernel, seq=s, ts=ts),
        out_shape=shapes,
        grid=(b, s // ts),
        in_specs=[
            pl.BlockSpec((None, POOL_HALO, D), lambda bi, i: (bi, jnp.maximum(i * r - 1, 0), 0)),
            pl.BlockSpec((None, ts, D), lambda bi, i: (bi, i, 0)),
            pl.BlockSpec((None, POOL_HALO, D), lambda bi, i: (bi, jnp.minimum((i + 1) * r, nh - 1), 0)),
            _mod_spec(0, rf), _mod_spec(1, rf), _mod_spec(2, rf), _mod_spec(3, rf), _mod_spec(4, rf),
            _full_spec((1, D)), _full_spec((1, D)),
            _full_spec(pw.shape), _full_spec((1, D)), _full_spec((1, D)),
        ] + _router_specs(),
        out_specs=specs,
        compiler_params=pltpu.CompilerParams(dimension_semantics=("arbitrary", "arbitrary"),
                                             vmem_limit_bytes=VMEM_LIMIT),
        name="pool_layer",
    )(x, x, x, m, m, m, m, m, n1g, n2g, pw, pb, pls, *router)


def _moe_kernel(h_ref, x1_ref, gates_ref, g2_ref, wg_ref, wu_ref, wd_ref, o_ref, acc_ref):
    g = pl.program_id(1)
    h = h_ref[...]
    gates = gates_ref[...]
    lane = lax.broadcasted_iota(jnp.int32, gates.shape, 1)
    parts = []
    for e in range(EPG):
        a = _dot(h, wg_ref[e])
        u = _dot(h, wu_ref[e])
        ge = jnp.sum(jnp.where(lane == N_GROUPS + EPG * g + e, gates, 0.0), axis=-1, keepdims=True)
        parts.append((_silu(a) * u * ge).astype(BF16))
    hid = jnp.concatenate(parts, axis=1)
    y = _dot(hid, wd_ref[...].reshape(EPG * D_EXPERT, D))

    @pl.when(g == 0)
    def _():
        acc_ref[...] = y

    @pl.when(g > 0)
    def _():
        acc_ref[...] += y

    @pl.when(g == N_GROUPS - 1)
    def _():
        o_ref[...] = x1_ref[...] + g2_ref[...] * acc_ref[...]


def _moe_layer(h2, x1, gates, m, row_fn, wg, wu, wd):
    b, s, _ = x1.shape
    n = b * s
    tm = min(1024, s)
    per_b = s // tm
    h2 = h2.reshape(n, D)
    x1 = x1.reshape(n, D)
    gates = gates.reshape(n, LANES)
    out = pl.pallas_call(
        _moe_kernel,
        out_shape=jax.ShapeDtypeStruct((n, D), F32),
        grid=(n // tm, N_GROUPS),
        in_specs=[
            pl.BlockSpec((tm, D), lambda t, g: (t, 0)),
            pl.BlockSpec((tm, D), lambda t, g: (t, 0)),
            pl.BlockSpec((tm, LANES), lambda t, g: (t, 0)),
            _mod_spec(5, lambda t, g: row_fn(t // per_b)),
            pl.BlockSpec((EPG, D, D_EXPERT), lambda t, g: (g, 0, 0)),
            pl.BlockSpec((EPG, D, D_EXPERT), lambda t, g: (g, 0, 0)),
            pl.BlockSpec((EPG, D_EXPERT, D), lambda t, g: (g, 0, 0)),
        ],
        out_specs=pl.BlockSpec((tm, D), lambda t, g: (t, 0)),
        scratch_shapes=[pltpu.VMEM((tm, D), F32)],
        compiler_params=pltpu.CompilerParams(dimension_semantics=("arbitrary", "arbitrary"),
                                             vmem_limit_bytes=VMEM_LIMIT),
        name="moe_layer",
    )(h2, x1, gates, m, wg, wu, wd)
    return out.reshape(b, s, D)


def _qkv_kernel(x_ref, sh1, sc1, n1g, w_ref, qg_ref, kg_ref, cos_ref, sin_ref, *out_refs, with_q):
    a = _rms_mod(x_ref[...], n1g[...], sh1[...], sc1[...]).astype(BF16)
    qkv = _dot(a, w_ref[...])
    cos, sin = cos_ref[...], sin_ref[...]
    lane = lax.broadcasted_iota(jnp.int32, cos.shape, 1)
    first_head = lane < HEAD_DIM
    first_half = (lane & 31) < 16

    def norm_rope(t, g):
        sq = t * t
        s_all = jnp.sum(sq, axis=-1, keepdims=True)
        s_lo = jnp.sum(jnp.where(first_head, sq, 0.0), axis=-1, keepdims=True)
        ms = jnp.where(first_head, s_lo, s_all - s_lo) * (1.0 / HEAD_DIM)
        tn = t * lax.rsqrt(ms + EPS) * g
        up = pltpu.roll(tn, LANES - 16, 1)
        dn = pltpu.roll(tn, 16, 1)
        return tn * cos + jnp.where(first_half, up, dn) * sin

    if with_q:
        q_ref, k_ref, v_ref = out_refs
        qg = qg_ref[...]
        for j in range(D // LANES):
            blk = norm_rope(qkv[:, j * LANES:(j + 1) * LANES], qg) * Q_SCALE
            q_ref[:, j * LANES:(j + 1) * LANES] = blk.astype(BF16)
        off = D
    else:
        k_ref, v_ref = out_refs
        off = 0
    kg = kg_ref[...]
    for j in range(KV_DIM // LANES):
        kn = norm_rope(qkv[:, off + j * LANES:off + (j + 1) * LANES], kg).astype(BF16)
        k_ref[2 * j] = kn[:, :HEAD_DIM]
        k_ref[2 * j + 1] = kn[:, HEAD_DIM:]
        vv = qkv[:, off + KV_DIM + j * LANES:off + KV_DIM + (j + 1) * LANES].astype(BF16)
        v_ref[2 * j] = vv[:, :HEAD_DIM]
        v_ref[2 * j + 1] = vv[:, HEAD_DIM:]


def _qkv_layer(x, m, row_fn, n1g, w, qg, kg, cos_t, sin_t, with_q):
    b, s, _ = x.shape
    ts = min(512, s)
    rf = lambda bi, i: row_fn(bi)
    kv_shape = jax.ShapeDtypeStruct((b, N_KV, s, HEAD_DIM), BF16)
    kv_spec = pl.BlockSpec((None, N_KV, ts, HEAD_DIM), lambda bi, i: (bi, 0, i, 0))
    shapes, specs = [kv_shape, kv_shape], [kv_spec, kv_spec]
    if with_q:
        shapes = [jax.ShapeDtypeStruct((b, s, D), BF16)] + shapes
        specs = [pl.BlockSpec((None, ts, D), lambda bi, i: (bi, i, 0))] + specs
    return pl.pallas_call(
        functools.partial(_qkv_kernel, with_q=with_q),
        out_shape=tuple(shapes),
        grid=(b, s // ts),
        in_specs=[
            pl.BlockSpec((None, ts, D), lambda bi, i: (bi, i, 0)),
            _mod_spec(0, rf), _mod_spec(1, rf),
            _full_spec((1, D)),
            _full_spec(w.shape),
            _full_spec((1, LANES)), _full_spec((1, LANES)),
            pl.BlockSpec((ts, LANES), lambda bi, i: (i, 0)),
            pl.BlockSpec((ts, LANES), lambda bi, i: (i, 0)),
        ],
        out_specs=tuple(specs),
        compiler_params=pltpu.CompilerParams(dimension_semantics=("arbitrary", "arbitrary"),
                                             vmem_limit_bytes=VMEM_LIMIT),
        name="qkv_layer",
    )(x, m, m, n1g, w, qg, kg, cos_t, sin_t)


def _attn_kernel(q_ref, k_ref, v_ref, o_ref, *, tq, tc):
    k, v = k_ref[...], v_ref[...]
    for r in range(tq // tc):
        for j in range(Q_PER_KV):
            rows, cols = pl.ds(r * tc, tc), pl.ds(j * HEAD_DIM, HEAD_DIM)
            s = lax.dot_general(q_ref[rows, cols], k, (((1,), (1,)), ((), ())), preferred_element_type=F32)
            m = jnp.max(s, axis=-1, keepdims=True)
            p = jnp.exp2(s - m)
            l = jnp.sum(p, axis=-1, keepdims=True)
            o = _dot(p.astype(BF16), v) / l
            o_ref[rows, cols] = o.astype(BF16)


def _attention(q, k_all, v_all):
    b, s, _ = q.shape
    nk = k_all.shape[2]
    tq, tc = 512, 256
    grp = Q_PER_KV * HEAD_DIM
    return pl.pallas_call(
        functools.partial(_attn_kernel, tq=tq, tc=tc),
        out_shape=jax.ShapeDtypeStruct((b, s, D), BF16),
        grid=(b, N_KV, s // tq),
        in_specs=[
            pl.BlockSpec((None, tq, grp), lambda bi, h, i: (bi, i, h)),
            pl.BlockSpec((None, None, nk, HEAD_DIM), lambda bi, h, i: (bi, h, 0, 0)),
            pl.BlockSpec((None, None, nk, HEAD_DIM), lambda bi, h, i: (bi, h, 0, 0)),
        ],
        out_specs=pl.BlockSpec((None, tq, grp), lambda bi, h, i: (bi, i, h)),
        compiler_params=pltpu.CompilerParams(dimension_semantics=("arbitrary", "arbitrary", "arbitrary"),
                                             vmem_limit_bytes=VMEM_LIMIT),
        name="attention",
    )(q, k_all, v_all)


def _proj_kernel(o_ref, x_ref, g1, sh2, sc2, n2g, wo_ref, wrh_ref, wrl_ref, br_ref,
                 x1_ref, h2_ref, gates_ref):
    y = _dot(o_ref[...], wo_ref[...])
    _post_mixer(x_ref[...], y, g1[...], n2g[...], sh2[...], sc2[...], wrh_ref, wrl_ref, br_ref,
                x1_ref, h2_ref, gates_ref)


def _proj_layer(o, x, m, row_fn, n2g, wo, router):
    b, s, _ = x.shape
    ts = min(512, s)
    rf = lambda bi, i: row_fn(bi)
    shapes, specs = _post_out((b, s), ts)
    return pl.pallas_call(
        _proj_kernel,
        out_shape=shapes,
        grid=(b, s // ts),
        in_specs=[
            pl.BlockSpec((None, ts, D), lambda bi, i: (bi, i, 0)),
            pl.BlockSpec((None, ts, D), lambda bi, i: (bi, i, 0)),
            _mod_spec(2, rf), _mod_spec(3, rf), _mod_spec(4, rf),
            _full_spec((1, D)),
            _full_spec((D, D)),
        ] + _router_specs(),
        out_specs=specs,
        compiler_params=pltpu.CompilerParams(dimension_semantics=("arbitrary", "arbitrary"),
                                             vmem_limit_bytes=VMEM_LIMIT),
        name="attn_proj",
    )(o, x, m, m, m, n2g, wo, *router)


def _gelu_tanh(x):
    c = 0.7978845608028654
    return x * (0.5 * (1.0 + jnp.tanh(c * (x + 0.044715 * (x * x * x)))))


def _sgu_kernel(x_ref, sh1, sc1, g1, sh2, sc2, n1g, n2g, win_ref, bin_ref, lng_ref, lnb_ref, ws_ref,
                bs_ref, wout_ref, wrh_ref, wrl_ref, br_ref, x1_ref, h2_ref, gates_ref, *, ts):
    x = x_ref[...]
    a = _rms_mod(x, n1g[...], sh1[...], sc1[...]).astype(BF16)
    z = _gelu_tanh(_dot(a, win_ref[...]) + bin_ref[...])
    u, v = z[:, :D], z[:, D:]
    mu = jnp.mean(v, axis=-1, keepdims=True)
    vc = v - mu
    var = jnp.mean(vc * vc, axis=-1, keepdims=True)
    vb = (vc * lax.rsqrt(var + EPS) * lng_ref[...] + lnb_ref[...]).astype(BF16)
    bias = bs_ref[...]
    rows = []
    for c in range(ts // CHUNK):
        cols = [_dot(ws_ref[g], vb[c * CHUNK:(c + 1) * CHUNK, g * LANES:(g + 1) * LANES])
                for g in range(SGU_G)]
        rows.append(jnp.concatenate(cols, axis=1) + bias)
    sv = jnp.concatenate(rows, axis=0)
    y = _dot((u * sv).astype(BF16), wout_ref[...])
    _post_mixer(x, y, g1[...], n2g[...], sh2[...], sc2[...], wrh_ref, wrl_ref, br_ref,
                x1_ref, h2_ref, gates_ref)


def _sgu_layer(x, m, row_fn, n1g, n2g, win, b_in, lng, lnb, ws, bs_full, wout, router):
    b, s, _ = x.shape
    ts = min(512, s)
    rf = lambda bi, i: row_fn(bi)
    shapes, specs = _post_out((b, s), ts)
    return pl.pallas_call(
        functools.partial(_sgu_kernel, ts=ts),
        out_shape=shapes,
        grid=(b, s // ts),
        in_specs=[
            pl.BlockSpec((None, ts, D), lambda bi, i: (bi, i, 0)),
            _mod_spec(0, rf), _mod_spec(1, rf), _mod_spec(2, rf), _mod_spec(3, rf), _mod_spec(4, rf),
            _full_spec((1, D)), _full_spec((1, D)),
            _full_spec((D, 2 * D)), _full_spec((1, 2 * D)),
            _full_spec((1, D)), _full_spec((1, D)),
            _full_spec((SGU_G, CHUNK, CHUNK)), _full_spec((CHUNK, D)),
            _full_spec((D, D)),
        ] + _router_specs(),
        out_specs=specs,
        compiler_params=pltpu.CompilerParams(dimension_semantics=("arbitrary", "arbitrary"),
                                             vmem_limit_bytes=VMEM_LIMIT),
        name="sgu_layer",
    )(x, m, m, m, m, m, n1g, n2g, win, b_in, lng, lnb, ws, bs_full, wout, *router)


def _rope_tables(s):
    t = jnp.arange(s)
    pos = jnp.stack([t // GRID_W, t % GRID_W], axis=-1).astype(F32)
    inv = ROPE_THETA ** (-jnp.arange(0, HEAD_DIM // 2, 2, dtype=F32) / (HEAD_DIM // 2))
    lane = jnp.arange(LANES)
    d = lane % HEAD_DIM
    ang = pos[:, d // 32] * inv[d % 16][None, :]
    sign = jnp.where((d % 32) < 16, -1.0, 1.0).astype(F32)
    return jnp.cos(ang), jnp.sin(ang) * sign[None, :]


def _router_params(w_rg, b_rg, w_re, b_re):
    w = jnp.concatenate([w_rg, w_re], axis=1)
    w = jnp.pad(w, ((0, 0), (0, LANES - w.shape[1])))
    hi = w.astype(BF16)
    lo = (w - hi.astype(F32)).astype(BF16)
    bias = jnp.concatenate([b_rg, b_re])
    bias = jnp.pad(bias, (0, LANES - bias.shape[0])).reshape(1, LANES)
    return hi, lo, bias


def kernel(x, c, ctx, c_ctx, w_mod, b_mod, norm1_g, norm2_g, pool_w, pool_b, pool_ls, attn_wqkv, attn_qg,
           attn_kg, attn_wo, sgu_win, sgu_bin, sgu_lng, sgu_lnb, sgu_ws, sgu_bs, sgu_wout, moe_wrg, moe_brg,
           moe_wre, moe_bre, moe_wg, moe_wu, moe_wd):
    b, s, _ = x.shape
    depth = w_mod.shape[0]
    assert b <= CTX_ROW and x.shape[2] == D and s % GRID_W == 0
    attn_layers = [i for i in range(depth) if i % N_MIX == 1]
    ctx_last = max(attn_layers, default=-1)

    cc = jnp.zeros((MOD_ROWS, D), F32).at[:b].set(c).at[CTX_ROW].set(c_ctx)
    mods = _adaln(cc, w_mod, b_mod).reshape(depth, MOD_ROWS, N_MOD, 1, D)
    lat_row = lambda bi: bi
    ctx_row = lambda bi: CTX_ROW
    cos_t, sin_t = _rope_tables(s)
    ones_t = jnp.ones((ctx.shape[1], LANES), F32)
    zeros_t = jnp.zeros((ctx.shape[1], LANES), F32)

    sctx = ctx
    for i in range(depth):
        kind, j = i % N_MIX, i // N_MIX
        ctx_in, ctx_upd = i <= ctx_last, i < ctx_last
        m = mods[i]
        n1g, n2g = norm1_g[i].reshape(1, D), norm2_g[i].reshape(1, D)
        router = _router_params(moe_wrg[i], moe_brg[i], moe_wre[i], moe_bre[i])
        wg, wu, wd = moe_wg[i].astype(BF16), moe_wu[i].astype(BF16), moe_wd[i].astype(BF16)
        streams = [(x, lat_row)] + ([(sctx, ctx_row)] if ctx_upd else [])
        outs = []
        if kind == 0:
            pw, pb = pool_w[j].astype(BF16), pool_b[j].reshape(1, D)
            pls = pool_ls[j].reshape(1, D)
            for xs, rf in streams:
                outs.append(_pool_layer(xs, m, rf, n1g, n2g, pw, pb, pls, router))
        elif kind == 1:
            w = attn_wqkv[j].astype(BF16)
            qg = jnp.tile(attn_qg[j], LANES // HEAD_DIM).reshape(1, LANES)
            kg = jnp.tile(attn_kg[j], LANES // HEAD_DIM).reshape(1, LANES)
            q, k, v = _qkv_layer(x, m, lat_row, n1g, w, qg, kg, cos_t, sin_t, True)
            k_c, v_c = _qkv_layer(sctx, m, ctx_row, n1g, w[:, D:], qg, kg, ones_t, zeros_t, False)
            o = _attention(q, jnp.concatenate([k_c, k], axis=2), jnp.concatenate([v_c, v], axis=2))
            outs.append(_proj_layer(o, x, m, lat_row, n2g, attn_wo[j].astype(BF16), router))
            assert not ctx_upd
        else:
            bs_full = jnp.repeat(sgu_bs[j].T, LANES, axis=1)
            args = (sgu_win[j].astype(BF16), sgu_bin[j].reshape(1, 2 * D), sgu_lng[j].reshape(1, D),
                    sgu_lnb[j].reshape(1, D), sgu_ws[j].astype(BF16), bs_full, sgu_wout[j].astype(BF16))
            for xs, rf in streams:
                outs.append(_sgu_layer(xs, m, rf, n1g, n2g, *args, router))
        new = [_moe_layer(h2, x1, gates, m, rf, wg, wu, wd)
               for (x1, h2, gates), (_, rf) in zip(outs, streams)]
        x = new[0]
        if ctx_upd:
            sctx = new[1]
    return x
```

```python
import functools

import jax
import jax.numpy as jnp
from jax import lax
from jax.experimental import pallas as pl
from jax.experimental.pallas import tpu as pltpu

F32 = jnp.float32
BF16 = jnp.bfloat16

D = 1024
GRID_W = 64
EPS = 1e-6
N_MIX = 3
POOL_WINDOWS = (2, 4, 8, 16)
POOL_CH = D // len(POOL_WINDOWS)
POOL_HALO = 8
HEAD_DIM = 64
N_KV = 4
Q_PER_KV = 4
KV_DIM = N_KV * HEAD_DIM
ROPE_THETA = 10000.0
Q_SCALE = HEAD_DIM ** -0.5 * 1.4426950408889634
CHUNK = 128
SGU_G = 8
N_GROUPS = 4
EPG = 4
D_EXPERT = 256
N_MOD = 6
LANES = 128
GROUP_LANE = 4
SORT_T = 512
RUN_ALIGN = 16
SORT_TS = SORT_T + N_GROUPS * RUN_ALIGN
RUN_BITS = tuple(SORT_T >> k for k in range(6))
MOE_TM = 512
CTX_ROW = 8
MOD_ROWS = 16
VMEM_LIMIT = 56 * 1024 * 1024


def _dot(a, b):
    return jnp.dot(a, b, preferred_element_type=F32)


def _rms_mod(x, g, sh, sc):
    ms = jnp.mean(x * x, axis=-1, keepdims=True)
    return x * lax.rsqrt(ms + EPS) * g * (1.0 + sc) + sh


def _silu(x):
    return x * (1.0 / (1.0 + jnp.exp(-x)))


def _route(h, wrh_ref, wrl_ref, br_ref):
    h_hi = h.astype(BF16)
    h_lo = (h - h_hi.astype(F32)).astype(BF16)
    whi = wrh_ref[...]
    lg = _dot(h_hi, whi) + _dot(h_hi, wrl_ref[...]) + _dot(h_lo, whi) + br_ref[...]
    lane = lax.broadcasted_iota(jnp.int32, lg.shape, 1)
    lane_f = lane.astype(F32)
    neg = -jnp.inf
    gmask = lane < N_GROUPS
    gl = jnp.where(gmask, lg, neg)
    gmax = jnp.max(gl, axis=-1, keepdims=True)
    gsel = jnp.min(jnp.where(gl == gmax, lane_f, float(LANES)), axis=-1, keepdims=True)
    gsum = jnp.sum(jnp.exp(gl - gmax), axis=-1, keepdims=True)
    g_w = 1.0 / gsum
    lo_lane = N_GROUPS + EPG * gsel
    emask = (lane_f >= lo_lane) & (lane_f < lo_lane + EPG)
    el = jnp.where(emask, lg, neg)
    t1 = jnp.max(el, axis=-1, keepdims=True)
    i1 = jnp.min(jnp.where(el == t1, lane_f, float(LANES)), axis=-1, keepdims=True)
    el2 = jnp.where(lane_f == i1, neg, el)
    t2 = jnp.max(el2, axis=-1, keepdims=True)
    i2 = jnp.min(jnp.where(el2 == t2, lane_f, float(LANES)), axis=-1, keepdims=True)
    e = jnp.exp(t2 - t1)
    w1 = 1.0 / (1.0 + e)
    w2 = e / (1.0 + e)
    rec = jnp.where(lane_f == i1 - lo_lane, w1 * g_w, jnp.where(lane_f == i2 - lo_lane, w2 * g_w, 0.0))
    return jnp.where(lane == GROUP_LANE, gsel, rec)


def _post_mixer(x, y, g1, n2g, sh2, sc2, wrh_ref, wrl_ref, br_ref, x1_ref, h2_ref, gates_ref):
    x1 = x + g1 * y
    h2 = _rms_mod(x1, n2g, sh2, sc2)
    x1_ref[...] = x1
    h2_ref[...] = h2.astype(BF16)
    gates_ref[...] = _route(h2, wrh_ref, wrl_ref, br_ref)


def _adaln_kernel(cc_ref, w_ref, b_ref, o_ref):
    s = _silu(cc_ref[...]).astype(BF16)
    o_ref[...] = _dot(s, w_ref[...].astype(BF16)) + b_ref[...]


def _adaln(cc, w_mod, b_mod):
    depth = w_mod.shape[0]
    return pl.pallas_call(
        _adaln_kernel,
        out_shape=jax.ShapeDtypeStruct((depth, MOD_ROWS, N_MOD * D), F32),
        grid=(depth, N_MOD),
        in_specs=[
            pl.BlockSpec((MOD_ROWS, D), lambda i, k: (0, 0)),
            pl.BlockSpec((None, D, D), lambda i, k: (i, 0, k)),
            pl.BlockSpec((None, 1, D), lambda i, k: (i, 0, k)),
        ],
        out_specs=pl.BlockSpec((None, MOD_ROWS, D), lambda i, k: (i, 0, k)),
        compiler_params=pltpu.CompilerParams(dimension_semantics=("arbitrary", "arbitrary")),
        name="adaln",
    )(cc, w_mod, b_mod.reshape(depth, 1, N_MOD * D))


def _mod_spec(k, row_fn):
    return pl.BlockSpec((None, None, 1, D), lambda *idx: (row_fn(*idx), k, 0, 0))


def _full_spec(shape):
    nd = len(shape)
    return pl.BlockSpec(shape, lambda *idx: (0,) * nd)


def _router_specs():
    return [_full_spec((D, LANES)), _full_spec((D, LANES)), _full_spec((1, LANES))]


def _post_out(rows_shape, ts):
    b, s = rows_shape
    shapes = (
        jax.ShapeDtypeStruct((b, s, D), F32),
        jax.ShapeDtypeStruct((b, s, D), BF16),
        jax.ShapeDtypeStruct((b, s, LANES), F32),
    )
    specs = (
        pl.BlockSpec((None, ts, D), lambda bi, i: (bi, i, 0)),
        pl.BlockSpec((None, ts, D), lambda bi, i: (bi, i, 0)),
        pl.BlockSpec((None, ts, LANES), lambda bi, i: (bi, i, 0)),
    )
    return shapes, specs


def _pool_kernel(xp_ref, xc_ref, xn_ref, sh1, sc1, g1, sh2, sc2, n1g, n2g, pw_ref, pb_ref, pls_ref,
                 wrh_ref, wrl_ref, br_ref, x1_ref, h2_ref, gates_ref, *, seq, ts):
    i = pl.program_id(1)
    nt = pl.num_programs(1)
    x = xc_ref[...]
    g, sh, sc = n1g[...], sh1[...], sc1[...]
    a_c = _rms_mod(x, g, sh, sc)
    a_p = _rms_mod(xp_ref[...], g, sh, sc) * (i > 0).astype(F32)
    a_n = _rms_mod(xn_ref[...], g, sh, sc) * (i < nt - 1).astype(F32)
    a_ext = jnp.concatenate([a_p, a_c, a_n], axis=0)
    n_ext = ts + 2 * POOL_HALO
    t = i * ts + lax.broadcasted_iota(jnp.int32, (ts, 1), 0)
    ys = []
    for gi, w in enumerate(POOL_WINDOWS):
        ag = a_ext[:, gi * POOL_CH:(gi + 1) * POOL_CH]
        acc = ag + pltpu.roll(ag, 1, 0)
        half = 1
        while 2 * half < w:
            acc = pltpu.roll(acc, half, 0) + pltpu.roll(acc, n_ext - half, 0)
            half *= 2
        win = acc[POOL_HALO:POOL_HALO + ts]
        cnt = (jnp.minimum(t + w // 2, seq) - jnp.maximum(t - w // 2, 0)).astype(F32)
        p = win / cnt - ag[POOL_HALO:POOL_HALO + ts]
        ys.append(_dot(p.astype(BF16), pw_ref[gi]))
    y = (jnp.concatenate(ys, axis=1) + pb_ref[...]) * pls_ref[...]
    _post_mixer(x, y, g1[...], n2g[...], sh2[...], sc2[...], wrh_ref, wrl_ref, br_ref,
                x1_ref, h2_ref, gates_ref)


def _pool_layer(x, m, row_fn, n1g, n2g, pw, pb, pls, router):
    b, s, _ = x.shape
    ts = min(512, s)
    nh = s // POOL_HALO
    r = ts // POOL_HALO
    shapes, specs = _post_out((b, s), ts)
    rf = lambda bi, i: row_fn(bi)
    return pl.pallas_call(
        functools.partial(_pool_kernel, seq=s, ts=ts),
        out_shape=shapes,
        grid=(b, s // ts),
        in_specs=[
            pl.BlockSpec((None, POOL_HALO, D), lambda bi, i: (bi, jnp.maximum(i * r - 1, 0), 0)),
            pl.BlockSpec((None, ts, D), lambda bi, i: (bi, i, 0)),
            pl.BlockSpec((None, POOL_HALO, D), lambda bi, i: (bi, jnp.minimum((i + 1) * r, nh - 1), 0)),
            _mod_spec(0, rf), _mod_spec(1, rf), _mod_spec(2, rf), _mod_spec(3, rf), _mod_spec(4, rf),
            _full_spec((1, D)), _full_spec((1, D)),
            _full_spec(pw.shape), _full_spec((1, D)), _full_spec((1, D)),
        ] + _router_specs(),
        out_specs=specs,
        compiler_params=pltpu.CompilerParams(dimension_semantics=("arbitrary", "arbitrary"),
                                             vmem_limit_bytes=VMEM_LIMIT),
        name="pool_layer",
    )(x, x, x, m, m, m, m, m, n1g, n2g, pw, pb, pls, *router)


def _run_copies(src, dst, src_off, dst_off, n, sem, wait, fixed_src=False):
    done = jnp.int32(0)
    for bit in RUN_BITS:
        take = (n & bit) != 0

        @pl.when(take)
        def _(done=done, bit=bit):
            so = pl.multiple_of(src_off + (0 if fixed_src else done), RUN_ALIGN)
            do = pl.multiple_of(dst_off + done, RUN_ALIGN)
            cp = pltpu.make_async_copy(src.at[pl.ds(so, bit)], dst.at[pl.ds(do, bit)], sem)
            if wait:
                cp.wait()
            else:
                cp.start()

        done = done + jnp.where(take, bit, 0)


def _dispatch_kernel(h_ref, r_ref, hs_ref, gs_ref, slot_ref, tab_ref, wtab_ref,
                     hbuf, gbuf, zh, zg, sems, cnt, pend, *, ntiles, rcap, nw):
    i = pl.program_id(0)
    sl = lax.rem(i, 2)
    t = SORT_T

    @pl.when(i == 0)
    def _():
        for g in range(N_GROUPS):
            cnt[g] = 0
        zh[...] = jnp.zeros_like(zh)
        zg[...] = jnp.zeros_like(zg)

    def runs(slot_idx, wait):
        for g in range(N_GROUPS):
            off, n, dst = pend[slot_idx, g], pend[slot_idx, N_GROUPS + g], pend[slot_idx, 2 * N_GROUPS + g]
            _run_copies(hbuf.at[slot_idx], hs_ref, off, dst, n, sems.at[slot_idx], wait)
            _run_copies(gbuf.at[slot_idx], gs_ref, off, dst, n, sems.at[slot_idx], wait)

    @pl.when(i >= 2)
    def _():
        runs(sl, True)

    r = r_ref[...]
    lane = lax.broadcasted_iota(jnp.int32, (t, LANES), 1)
    lane_f = lane.astype(F32)
    onehot = jnp.where(lane_f == r[:, GROUP_LANE:GROUP_LANE + 1], 1.0, 0.0)
    before = lax.broadcasted_iota(jnp.int32, (t, t), 1) < lax.broadcasted_iota(jnp.int32, (t, t), 0)
    rank = _dot(jnp.where(before, 1.0, 0.0).astype(BF16), onehot.astype(BF16))
    counts = jnp.sum(onehot, axis=0, keepdims=True)
    off = jnp.int32(0)
    off_lane = jnp.zeros((1, LANES), F32)
    for g in range(N_GROUPS):
        c = counts[0, g].astype(jnp.int32)
        cpad = ((c + (RUN_ALIGN - 1)) // RUN_ALIGN) * RUN_ALIGN
        pend[sl, g] = off
        pend[sl, N_GROUPS + g] = cpad
        pend[sl, 2 * N_GROUPS + g] = g * rcap + cnt[g]
        tab_ref[i, g] = cnt[g]
        tab_ref[i, N_GROUPS + g] = cpad
        cnt[g] = cnt[g] + cpad
        off_lane = jnp.where(lane[:1] == g, off.astype(F32), off_lane)
        off = off + cpad
    slot = jnp.sum(onehot * (rank + off_lane), axis=-1, keepdims=True)
    slot_b = jnp.broadcast_to(slot, (t, LANES))
    slot_ref[...] = slot_b
    slot_row = slot_b.T[:1]
    srow = lax.broadcasted_iota(jnp.int32, (SORT_TS, t), 0).astype(F32)
    perm = jnp.where(srow == slot_row, 1.0, 0.0).astype(BF16)
    hbuf[sl] = _dot(perm, h_ref[...]).astype(BF16)
    r_hi = r.astype(BF16)
    r_1 = r - r_hi.astype(F32)
    r_mid = r_1.astype(BF16)
    r_lo = (r_1 - r_mid.astype(F32)).astype(BF16)
    gbuf[sl] = _dot(perm, r_hi) + _dot(perm, r_mid) + _dot(perm, r_lo)
    runs(sl, False)

    @pl.when(i == ntiles - 1)
    def _():
        if ntiles >= 2:
            runs(1 - sl, True)
        runs(sl, True)
        blocks_per_region = rcap // MOE_TM
        starts, total = [], jnp.int32(0)
        for wait in (False, True):
            for g in range(N_GROUPS):
                c = cnt[g]
                nt = (c + (MOE_TM - 1)) // MOE_TM
                for buf, dst in ((zh, hs_ref), (zg, gs_ref)):
                    _run_copies(buf, dst, 0, g * rcap + c, nt * MOE_TM - c, sems.at[0], wait, fixed_src=True)
                if not wait:
                    starts.append(total)
                    total = total + nt

        def item(w, carry):
            wc = jnp.minimum(w, total - 1)
            grp = sum((wc >= starts[g]).astype(jnp.int32) for g in range(1, N_GROUPS))
            first = jnp.int32(0)
            for g in range(1, N_GROUPS):
                first = jnp.where(grp == g, starts[g], first)
            blk = grp * blocks_per_region + (wc - first)
            valid = (w < total).astype(jnp.int32)
            wtab_ref[0, w] = blk
            wtab_ref[1, w] = jnp.where(valid == 1, blk, N_GROUPS * blocks_per_region)
            wtab_ref[2, w] = grp
            wtab_ref[3, w] = valid
            return carry

        lax.fori_loop(0, nw, item, 0)


def _moe_kernel(wtab, hs_ref, gs_ref, wg_ref, wu_ref, wd_ref, ys_ref):
    w = pl.program_id(0)

    @pl.when(wtab[3, w] == 1)
    def _():
        h = hs_ref[...]
        gates = gs_ref[...]
        parts = []
        for e in range(EPG):
            a = _dot(h, wg_ref[e])
            u = _dot(h, wu_ref[e])
            parts.append((_silu(a) * u * gates[:, e:e + 1]).astype(BF16))
        hid = jnp.concatenate(parts, axis=1)
        ys_ref[...] = _dot(hid, wd_ref[...].reshape(EPG * D_EXPERT, D)).astype(BF16)

    @pl.when(wtab[3, w] == 0)
    def _():
        ys_ref[...] = jnp.zeros_like(ys_ref)


def _combine_kernel(tab, x1_ref, slot_ref, g2_ref, ys_ref, o_ref, ybuf, sems, *, ntiles, rcap):
    i = pl.program_id(0)
    sl = lax.rem(i, 2)

    def gather(tile, slot_idx, wait):
        off = jnp.int32(0)
        for g in range(N_GROUPS):
            n = tab[tile, N_GROUPS + g]
            _run_copies(ys_ref, ybuf.at[slot_idx], g * rcap + tab[tile, g], off, n, sems.at[slot_idx], wait)
            off = off + n

    @pl.when(i == 0)
    def _():
        ybuf[...] = jnp.zeros_like(ybuf)
        gather(0, 0, False)

    @pl.when(i + 1 < ntiles)
    def _():
        gather(i + 1, 1 - sl, False)

    gather(i, sl, True)
    lane_s = lax.broadcasted_iota(jnp.int32, (SORT_T, SORT_TS), 1).astype(F32)
    unperm = jnp.where(lane_s == slot_ref[:, :1], 1.0, 0.0).astype(BF16)
    o_ref[...] = x1_ref[...] + g2_ref[...] * _dot(unperm, ybuf[sl])


def _moe_layer(h2, x1, route, m, row_fn, wg, wu, wd):
    b, s, _ = x1.shape
    n = b * s
    assert n % SORT_T == 0 and n % MOE_TM == 0
    ntiles = n // SORT_T
    rcap = n
    rtot = N_GROUPS * rcap + MOE_TM
    nw = -(-ntiles * SORT_TS // MOE_TM) + N_GROUPS
    arb = pltpu.CompilerParams(dimension_semantics=("arbitrary",), vmem_limit_bytes=VMEM_LIMIT)
    any_spec = pl.BlockSpec(memory_space=pl.ANY)
    smem_spec = pl.BlockSpec(memory_space=pltpu.SMEM)
    hs, gs, slot, tab, wtab = pl.pallas_call(
        functools.partial(_dispatch_kernel, ntiles=ntiles, rcap=rcap, nw=nw),
        out_shape=(
            jax.ShapeDtypeStruct((rtot, D), BF16),
            jax.ShapeDtypeStruct((rtot, LANES), F32),
            jax.ShapeDtypeStruct((n, LANES), F32),
            jax.ShapeDtypeStruct((ntiles, 2 * N_GROUPS), jnp.int32),
            jax.ShapeDtypeStruct((4, nw), jnp.int32),
        ),
        grid=(ntiles,),
        in_specs=[pl.BlockSpec((SORT_T, D), lambda i: (i, 0)), pl.BlockSpec((SORT_T, LANES), lambda i: (i, 0))],
        out_specs=(any_spec, any_spec, pl.BlockSpec((SORT_T, LANES), lambda i: (i, 0)), smem_spec, smem_spec),
        scratch_shapes=[
            pltpu.VMEM((2, SORT_TS, D), BF16),
            pltpu.VMEM((2, SORT_TS, LANES), F32),
            pltpu.VMEM((MOE_TM // 2, D), BF16),
            pltpu.VMEM((MOE_TM // 2, LANES), F32),
            pltpu.SemaphoreType.DMA((2,)),
            pltpu.SMEM((N_GROUPS,), jnp.int32),
            pltpu.SMEM((2, 3 * N_GROUPS), jnp.int32),
        ],
        compiler_params=arb,
        name="moe_dispatch",
    )(h2.reshape(n, D), route.reshape(n, LANES))

    ys = pl.pallas_call(
        _moe_kernel,
        out_shape=jax.ShapeDtypeStruct((rtot, D), BF16),
        grid_spec=pltpu.PrefetchScalarGridSpec(
            num_scalar_prefetch=1,
            grid=(nw,),
            in_specs=[
                pl.BlockSpec((MOE_TM, D), lambda w, tb: (tb[0, w], 0)),
                pl.BlockSpec((MOE_TM, LANES), lambda w, tb: (tb[0, w], 0)),
                pl.BlockSpec((EPG, D, D_EXPERT), lambda w, tb: (tb[2, w], 0, 0)),
                pl.BlockSpec((EPG, D, D_EXPERT), lambda w, tb: (tb[2, w], 0, 0)),
                pl.BlockSpec((EPG, D_EXPERT, D), lambda w, tb: (tb[2, w], 0, 0)),
            ],
            out_specs=pl.BlockSpec((MOE_TM, D), lambda w, tb: (tb[1, w], 0)),
        ),
        compiler_params=arb,
        name="moe_experts",
    )(wtab, hs, gs, wg, wu, wd)

    out = pl.pallas_call(
        functools.partial(_combine_kernel, ntiles=ntiles, rcap=rcap),
        out_shape=jax.ShapeDtypeStruct((n, D), F32),
        grid_spec=pltpu.PrefetchScalarGridSpec(
            num_scalar_prefetch=1,
            grid=(ntiles,),
            in_specs=[
                pl.BlockSpec((SORT_T, D), lambda i, tb: (i, 0)),
                pl.BlockSpec((SORT_T, LANES), lambda i, tb: (i, 0)),
                _mod_spec(5, lambda i, tb: row_fn(i * SORT_T // s)),
                any_spec,
            ],
            out_specs=pl.BlockSpec((SORT_T, D), lambda i, tb: (i, 0)),
            scratch_shapes=[pltpu.VMEM((2, SORT_TS, D), BF16), pltpu.SemaphoreType.DMA((2,))],
        ),
        compiler_params=arb,
        name="moe_combine",
    )(tab, x1.reshape(n, D), slot, m, ys)
    return out.reshape(b, s, D)


def _qkv_kernel(x_ref, sh1, sc1, n1g, w_ref, qg_ref, kg_ref, cos_ref, sin_ref, *out_refs, with_q):
    a = _rms_mod(x_ref[...], n1g[...], sh1[...], sc1[...]).astype(BF16)
    qkv = _dot(a, w_ref[...])
    cos, sin = cos_ref[...], sin_ref[...]
    lane = lax.broadcasted_iota(jnp.int32, cos.shape, 1)
    first_head = lane < HEAD_DIM
    first_half = (lane & 31) < 16

    def norm_rope(t, g):
        sq = t * t
        s_all = jnp.sum(sq, axis=-1, keepdims=True)
        s_lo = jnp.sum(jnp.where(first_head, sq, 0.0), axis=-1, keepdims=True)
        ms = jnp.where(first_head, s_lo, s_all - s_lo) * (1.0 / HEAD_DIM)
        tn = t * lax.rsqrt(ms + EPS) * g
        up = pltpu.roll(tn, LANES - 16, 1)
        dn = pltpu.roll(tn, 16, 1)
        return tn * cos + jnp.where(first_half, up, dn) * sin

    if with_q:
        q_ref, k_ref, v_ref = out_refs
        qg = qg_ref[...]
        for j in range(D // LANES):
            blk = norm_rope(qkv[:, j * LANES:(j + 1) * LANES], qg) * Q_SCALE
            q_ref[:, j * LANES:(j + 1) * LANES] = blk.astype(BF16)
        off = D
    else:
        k_ref, v_ref = out_refs
        off = 0
    kg = kg_ref[...]
    for j in range(KV_DIM // LANES):
        kn = norm_rope(qkv[:, off + j * LANES:off + (j + 1) * LANES], kg).astype(BF16)
        k_ref[2 * j] = kn[:, :HEAD_DIM]
        k_ref[2 * j + 1] = kn[:, HEAD_DIM:]
        vv = qkv[:, off + KV_DIM + j * LANES:off + KV_DIM + (j + 1) * LANES].astype(BF16)
        v_ref[2 * j] = vv[:, :HEAD_DIM]
        v_ref[2 * j + 1] = vv[:, HEAD_DIM:]


def _qkv_layer(x, m, row_fn, n1g, w, qg, kg, cos_t, sin_t, with_q):
    b, s, _ = x.shape
    ts = min(512, s)
    rf = lambda bi, i: row_fn(bi)
    kv_shape = jax.ShapeDtypeStruct((b, N_KV, s, HEAD_DIM), BF16)
    kv_spec = pl.BlockSpec((None, N_KV, ts, HEAD_DIM), lambda bi, i: (bi, 0, i, 0))
    shapes, specs = [kv_shape, kv_shape], [kv_spec, kv_spec]
    if with_q:
        shapes = [jax.ShapeDtypeStruct((b, s, D), BF16)] + shapes
        specs = [pl.BlockSpec((None, ts, D), lambda bi, i: (bi, i, 0))] + specs
    return pl.pallas_call(
        functools.partial(_qkv_kernel, with_q=with_q),
        out_shape=tuple(shapes),
        grid=(b, s // ts),
        in_specs=[
            pl.BlockSpec((None, ts, D), lambda bi, i: (bi, i, 0)),
            _mod_spec(0, rf), _mod_spec(1, rf),
            _full_spec((1, D)),
            _full_spec(w.shape),
            _full_spec((1, LANES)), _full_spec((1, LANES)),
            pl.BlockSpec((ts, LANES), lambda bi, i: (i, 0)),
            pl.BlockSpec((ts, LANES), lambda bi, i: (i, 0)),
        ],
        out_specs=tuple(specs),
        compiler_params=pltpu.CompilerParams(dimension_semantics=("arbitrary", "arbitrary"),
                                             vmem_limit_bytes=VMEM_LIMIT),
        name="qkv_layer",
    )(x, m, m, n1g, w, qg, kg, cos_t, sin_t)


def _attn_kernel(q_ref, k_ref, v_ref, o_ref, *, tq, tc):
    k, v = k_ref[...], v_ref[...]
    for r in range(tq // tc):
        for j in range(Q_PER_KV):
            rows, cols = pl.ds(r * tc, tc), pl.ds(j * HEAD_DIM, HEAD_DIM)
            s = lax.dot_general(q_ref[rows, cols], k, (((1,), (1,)), ((), ())), preferred_element_type=F32)
            m = jnp.max(s, axis=-1, keepdims=True)
            p = jnp.exp2(s - m)
            l = jnp.sum(p, axis=-1, keepdims=True)
            o = _dot(p.astype(BF16), v) / l
            o_ref[rows, cols] = o.astype(BF16)


def _attention(q, k_all, v_all):
    b, s, _ = q.shape
    nk = k_all.shape[2]
    tq, tc = 512, 256
    grp = Q_PER_KV * HEAD_DIM
    return pl.pallas_call(
        functools.partial(_attn_kernel, tq=tq, tc=tc),
        out_shape=jax.ShapeDtypeStruct((b, s, D), BF16),
        grid=(b, N_KV, s // tq),
        in_specs=[
            pl.BlockSpec((None, tq, grp), lambda bi, h, i: (bi, i, h)),
            pl.BlockSpec((None, None, nk, HEAD_DIM), lambda bi, h, i: (bi, h, 0, 0)),
            pl.BlockSpec((None, None, nk, HEAD_DIM), lambda bi, h, i: (bi, h, 0, 0)),
        ],
        out_specs=pl.BlockSpec((None, tq, grp), lambda bi, h, i: (bi, i, h)),
        compiler_params=pltpu.CompilerParams(dimension_semantics=("arbitrary", "arbitrary", "arbitrary"),
                                             vmem_limit_bytes=VMEM_LIMIT),
        name="attention",
    )(q, k_all, v_all)


def _proj_kernel(o_ref, x_ref, g1, sh2, sc2, n2g, wo_ref, wrh_ref, wrl_ref, br_ref,
                 x1_ref, h2_ref, gates_ref):
    y = _dot(o_ref[...], wo_ref[...])
    _post_mixer(x_ref[...], y, g1[...], n2g[...], sh2[...], sc2[...], wrh_ref, wrl_ref, br_ref,
                x1_ref, h2_ref, gates_ref)


def _proj_layer(o, x, m, row_fn, n2g, wo, router):
    b, s, _ = x.shape
    ts = min(512, s)
    rf = lambda bi, i: row_fn(bi)
    shapes, specs = _post_out((b, s), ts)
    return pl.pallas_call(
        _proj_kernel,
        out_shape=shapes,
        grid=(b, s // ts),
        in_specs=[
            pl.BlockSpec((None, ts, D), lambda bi, i: (bi, i, 0)),
            pl.BlockSpec((None, ts, D), lambda bi, i: (bi, i, 0)),
            _mod_spec(2, rf), _mod_spec(3, rf), _mod_spec(4, rf),
            _full_spec((1, D)),
            _full_spec((D, D)),
        ] + _router_specs(),
        out_specs=specs,
        compiler_params=pltpu.CompilerParams(dimension_semantics=("arbitrary", "arbitrary"),
                                             vmem_limit_bytes=VMEM_LIMIT),
        name="attn_proj",
    )(o, x, m, m, m, n2g, wo, *router)


def _gelu_tanh(x):
    c = 0.7978845608028654
    return x * (0.5 * (1.0 + jnp.tanh(c * (x + 0.044715 * (x * x * x)))))


def _sgu_kernel(x_ref, sh1, sc1, g1, sh2, sc2, n1g, n2g, win_ref, bin_ref, lng_ref, lnb_ref, ws_ref,
                bs_ref, wout_ref, wrh_ref, wrl_ref, br_ref, x1_ref, h2_ref, gates_ref, *, ts):
    x = x_ref[...]
    a = _rms_mod(x, n1g[...], sh1[...], sc1[...]).astype(BF16)
    z = _gelu_tanh(_dot(a, win_ref[...]) + bin_ref[...])
    u, v = z[:, :D], z[:, D:]
    mu = jnp.mean(v, axis=-1, keepdims=True)
    vc = v - mu
    var = jnp.mean(vc * vc, axis=-1, keepdims=True)
    vb = (vc * lax.rsqrt(var + EPS) * lng_ref[...] + lnb_ref[...]).astype(BF16)
    bias = bs_ref[...]
    rows = []
    for c in range(ts // CHUNK):
        cols = [_dot(ws_ref[g], vb[c * CHUNK:(c + 1) * CHUNK, g * LANES:(g + 1) * LANES])
                for g in range(SGU_G)]
        rows.append(jnp.concatenate(cols, axis=1) + bias)
    sv = jnp.concatenate(rows, axis=0)
    y = _dot((u * sv).astype(BF16), wout_ref[...])
    _post_mixer(x, y, g1[...], n2g[...], sh2[...], sc2[...], wrh_ref, wrl_ref, br_ref,
                x1_ref, h2_ref, gates_ref)


def _sgu_layer(x, m, row_fn, n1g, n2g, win, b_in, lng, lnb, ws, bs_full, wout, router):
    b, s, _ = x.shape
    ts = min(512, s)
    rf = lambda bi, i: row_fn(bi)
    shapes, specs = _post_out((b, s), ts)
    return pl.pallas_call(
        functools.partial(_sgu_kernel, ts=ts),
        out_shape=shapes,
        grid=(b, s // ts),
        in_specs=[
            pl.BlockSpec((None, ts, D), lambda bi, i: (bi, i, 0)),
            _mod_spec(0, rf), _mod_spec(1, rf), _mod_spec(2, rf), _mod_spec(3, rf), _mod_spec(4, rf),
            _full_spec((1, D)), _full_spec((1, D)),
            _full_spec((D, 2 * D)), _full_spec((1, 2 * D)),
            _full_spec((1, D)), _full_spec((1, D)),
            _full_spec((SGU_G, CHUNK, CHUNK)), _full_spec((CHUNK, D)),
            _full_spec((D, D)),
        ] + _router_specs(),
        out_specs=specs,
        compiler_params=pltpu.CompilerParams(dimension_semantics=("arbitrary", "arbitrary"),
                                             vmem_limit_bytes=VMEM_LIMIT),
        name="sgu_layer",
    )(x, m, m, m, m, m, n1g, n2g, win, b_in, lng, lnb, ws, bs_full, wout, *router)


def _rope_tables(s):
    t = jnp.arange(s)
    pos = jnp.stack([t // GRID_W, t % GRID_W], axis=-1).astype(F32)
    inv = ROPE_THETA ** (-jnp.arange(0, HEAD_DIM // 2, 2, dtype=F32) / (HEAD_DIM // 2))
    lane = jnp.arange(LANES)
    d = lane % HEAD_DIM
    ang = pos[:, d // 32] * inv[d % 16][None, :]
    sign = jnp.where((d % 32) < 16, -1.0, 1.0).astype(F32)
    return jnp.cos(ang), jnp.sin(ang) * sign[None, :]


def _router_params(w_rg, b_rg, w_re, b_re):
    w = jnp.concatenate([w_rg, w_re], axis=1)
    w = jnp.pad(w, ((0, 0), (0, LANES - w.shape[1])))
    hi = w.astype(BF16)
    lo = (w - hi.astype(F32)).astype(BF16)
    bias = jnp.concatenate([b_rg, b_re])
    bias = jnp.pad(bias, (0, LANES - bias.shape[0])).reshape(1, LANES)
    return hi, lo, bias


def kernel(x, c, ctx, c_ctx, w_mod, b_mod, norm1_g, norm2_g, pool_w, pool_b, pool_ls, attn_wqkv, attn_qg,
           attn_kg, attn_wo, sgu_win, sgu_bin, sgu_lng, sgu_lnb, sgu_ws, sgu_bs, sgu_wout, moe_wrg, moe_brg,
           moe_wre, moe_bre, moe_wg, moe_wu, moe_wd):
    b, s, _ = x.shape
    depth = w_mod.shape[0]
    assert b <= CTX_ROW and x.shape[2] == D and s % GRID_W == 0
    attn_layers = [i for i in range(depth) if i % N_MIX == 1]
    ctx_last = max(attn_layers, default=-1)

    cc = jnp.zeros((MOD_ROWS, D), F32).at[:b].set(c).at[CTX_ROW].set(c_ctx)
    mods = _adaln(cc, w_mod, b_mod).reshape(depth, MOD_ROWS, N_MOD, 1, D)
    lat_row = lambda bi: bi
    ctx_row = lambda bi: CTX_ROW
    cos_t, sin_t = _rope_tables(s)
    ones_t = jnp.ones((ctx.shape[1], LANES), F32)
    zeros_t = jnp.zeros((ctx.shape[1], LANES), F32)

    sctx = ctx
    for i in range(depth):
        kind, j = i % N_MIX, i // N_MIX
        ctx_in, ctx_upd = i <= ctx_last, i < ctx_last
        m = mods[i]
        n1g, n2g = norm1_g[i].reshape(1, D), norm2_g[i].reshape(1, D)
        router = _router_params(moe_wrg[i], moe_brg[i], moe_wre[i], moe_bre[i])
        wg, wu, wd = moe_wg[i].astype(BF16), moe_wu[i].astype(BF16), moe_wd[i].astype(BF16)
        streams = [(x, lat_row)] + ([(sctx, ctx_row)] if ctx_upd else [])
        outs = []
        if kind == 0:
            pw, pb = pool_w[j].astype(BF16), pool_b[j].reshape(1, D)
            pls = pool_ls[j].reshape(1, D)
            for xs, rf in streams:
                outs.append(_pool_layer(xs, m, rf, n1g, n2g, pw, pb, pls, router))
        elif kind == 1:
            w = attn_wqkv[j].astype(BF16)
            qg = jnp.tile(attn_qg[j], LANES // HEAD_DIM).reshape(1, LANES)
            kg = jnp.tile(attn_kg[j], LANES // HEAD_DIM).reshape(1, LANES)
            q, k, v = _qkv_layer(x, m, lat_row, n1g, w, qg, kg, cos_t, sin_t, True)
            k_c, v_c = _qkv_layer(sctx, m, ctx_row, n1g, w[:, D:], qg, kg, ones_t, zeros_t, False)
            o = _attention(q, jnp.concatenate([k_c, k], axis=2), jnp.concatenate([v_c, v], axis=2))
            outs.append(_proj_layer(o, x, m, lat_row, n2g, attn_wo[j].astype(BF16), router))
            assert not ctx_upd
        else:
            bs_full = jnp.repeat(sgu_bs[j].T, LANES, axis=1)
            args = (sgu_win[j].astype(BF16), sgu_bin[j].reshape(1, 2 * D), sgu_lng[j].reshape(1, D),
                    sgu_lnb[j].reshape(1, D), sgu_ws[j].astype(BF16), bs_full, sgu_wout[j].astype(BF16))
            for xs, rf in streams:
                outs.append(_sgu_layer(xs, m, rf, n1g, n2g, *args, router))
        new = [_moe_layer(h2, x1, gates, m, rf, wg, wu, wd)
               for (x1, h2, gates), (_, rf) in zip(outs, streams)]
        x = new[0]
        if ctx_upd:
            sctx = new[1]
    return x
```

```python
import functools

import jax
import jax.numpy as jnp
from jax import lax
from jax.experimental import pallas as pl
from jax.experimental.pallas import tpu as pltpu

F32 = jnp.float32
BF16 = jnp.bfloat16

D = 1024
GRID_W = 64
EPS = 1e-6
N_MIX = 3
POOL_WINDOWS = (2, 4, 8, 16)
POOL_CH = D // len(POOL_WINDOWS)
POOL_HALO = 8
HEAD_DIM = 64
N_KV = 4
Q_PER_KV = 4
KV_DIM = N_KV * HEAD_DIM
ROPE_THETA = 10000.0
Q_SCALE = HEAD_DIM ** -0.5 * 1.4426950408889634
CHUNK = 128
SGU_G = 8
N_GROUPS = 4
EPG = 4
D_EXPERT = 256
N_MOD = 6
LANES = 128
GROUP_LANE = 4
GATE_PIECE = 8
SORT_T = 512
RUN_ALIGN = 16
SORT_TS = SORT_T + N_GROUPS * RUN_ALIGN
RUN_BITS = tuple(SORT_T >> k for k in range(6))
MOE_TM = 512
CTX_ROW = 8
MOD_ROWS = 16
VMEM_LIMIT = 56 * 1024 * 1024


def _dot(a, b):
    return jnp.dot(a, b, preferred_element_type=F32)


def _rms_mod(x, g, sh, sc):
    ms = jnp.mean(x * x, axis=-1, keepdims=True)
    return x * lax.rsqrt(ms + EPS) * g * (1.0 + sc) + sh


def _silu(x):
    return x * (1.0 / (1.0 + jnp.exp(-x)))


def _route(h, wrh_ref, wrl_ref, br_ref):
    h_hi = h.astype(BF16)
    h_lo = (h - h_hi.astype(F32)).astype(BF16)
    whi = wrh_ref[...]
    lg = _dot(h_hi, whi) + _dot(h_hi, wrl_ref[...]) + _dot(h_lo, whi) + br_ref[...]
    lane = lax.broadcasted_iota(jnp.int32, lg.shape, 1)
    lane_f = lane.astype(F32)
    neg = -jnp.inf
    gmask = lane < N_GROUPS
    gl = jnp.where(gmask, lg, neg)
    gmax = jnp.max(gl, axis=-1, keepdims=True)
    gsel = jnp.min(jnp.where(gl == gmax, lane_f, float(LANES)), axis=-1, keepdims=True)
    gsum = jnp.sum(jnp.exp(gl - gmax), axis=-1, keepdims=True)
    g_w = 1.0 / gsum
    lo_lane = N_GROUPS + EPG * gsel
    emask = (lane_f >= lo_lane) & (lane_f < lo_lane + EPG)
    el = jnp.where(emask, lg, neg)
    t1 = jnp.max(el, axis=-1, keepdims=True)
    i1 = jnp.min(jnp.where(el == t1, lane_f, float(LANES)), axis=-1, keepdims=True)
    el2 = jnp.where(lane_f == i1, neg, el)
    t2 = jnp.max(el2, axis=-1, keepdims=True)
    i2 = jnp.min(jnp.where(el2 == t2, lane_f, float(LANES)), axis=-1, keepdims=True)
    e = jnp.exp(t2 - t1)
    w1 = 1.0 / (1.0 + e)
    w2 = e / (1.0 + e)
    rec = jnp.where(lane_f == i1 - lo_lane, w1 * g_w, jnp.where(lane_f == i2 - lo_lane, w2 * g_w, 0.0))
    return jnp.where(lane == GROUP_LANE, gsel, rec)


def _post_mixer(x, y, g1, n2g, sh2, sc2, wrh_ref, wrl_ref, br_ref, x1_ref, h2_ref, gates_ref):
    x1 = x + g1 * y
    h2 = _rms_mod(x1, n2g, sh2, sc2)
    x1_ref[...] = x1
    h2_ref[...] = h2.astype(BF16)
    gates_ref[...] = _route(h2, wrh_ref, wrl_ref, br_ref)


def _adaln_kernel(cc_ref, w_ref, b_ref, o_ref):
    s = _silu(cc_ref[...]).astype(BF16)
    o_ref[...] = _dot(s, w_ref[...].astype(BF16)) + b_ref[...]


def _adaln(cc, w_mod, b_mod):
    depth = w_mod.shape[0]
    return pl.pallas_call(
        _adaln_kernel,
        out_shape=jax.ShapeDtypeStruct((depth, MOD_ROWS, N_MOD * D), F32),
        grid=(depth, N_MOD),
        in_specs=[
            pl.BlockSpec((MOD_ROWS, D), lambda i, k: (0, 0)),
            pl.BlockSpec((None, D, D), lambda i, k: (i, 0, k)),
            pl.BlockSpec((None, 1, D), lambda i, k: (i, 0, k)),
        ],
        out_specs=pl.BlockSpec((None, MOD_ROWS, D), lambda i, k: (i, 0, k)),
        compiler_params=pltpu.CompilerParams(dimension_semantics=("arbitrary", "arbitrary")),
        name="adaln",
    )(cc, w_mod, b_mod.reshape(depth, 1, N_MOD * D))


def _mod_spec(k, row_fn):
    return pl.BlockSpec((None, None, 1, D), lambda *idx: (row_fn(*idx), k, 0, 0))


def _full_spec(shape):
    nd = len(shape)
    return pl.BlockSpec(shape, lambda *idx: (0,) * nd)


def _router_specs():
    return [_full_spec((D, LANES)), _full_spec((D, LANES)), _full_spec((1, LANES))]


def _post_out(rows_shape, ts):
    b, s = rows_shape
    shapes = (
        jax.ShapeDtypeStruct((b, s, D), F32),
        jax.ShapeDtypeStruct((b, s, D), BF16),
        jax.ShapeDtypeStruct((b, s, LANES), F32),
    )
    specs = (
        pl.BlockSpec((None, ts, D), lambda bi, i: (bi, i, 0)),
        pl.BlockSpec((None, ts, D), lambda bi, i: (bi, i, 0)),
        pl.BlockSpec((None, ts, LANES), lambda bi, i: (bi, i, 0)),
    )
    return shapes, specs


def _pool_kernel(xp_ref, xc_ref, xn_ref, sh1, sc1, g1, sh2, sc2, n1g, n2g, pw_ref, pb_ref, pls_ref,
                 wrh_ref, wrl_ref, br_ref, x1_ref, h2_ref, gates_ref, *, seq, ts):
    i = pl.program_id(1)
    nt = pl.num_programs(1)
    x = xc_ref[...]
    g, sh, sc = n1g[...], sh1[...], sc1[...]
    a_c = _rms_mod(x, g, sh, sc)
    a_p = _rms_mod(xp_ref[...], g, sh, sc) * (i > 0).astype(F32)
    a_n = _rms_mod(xn_ref[...], g, sh, sc) * (i < nt - 1).astype(F32)
    a_ext = jnp.concatenate([a_p, a_c, a_n], axis=0)
    n_ext = ts + 2 * POOL_HALO
    t = i * ts + lax.broadcasted_iota(jnp.int32, (ts, 1), 0)
    ys = []
    for gi, w in enumerate(POOL_WINDOWS):
        ag = a_ext[:, gi * POOL_CH:(gi + 1) * POOL_CH]
        acc = ag + pltpu.roll(ag, 1, 0)
        half = 1
        while 2 * half < w:
            acc = pltpu.roll(acc, half, 0) + pltpu.roll(acc, n_ext - half, 0)
            half *= 2
        win = acc[POOL_HALO:POOL_HALO + ts]
        cnt = (jnp.minimum(t + w // 2, seq) - jnp.maximum(t - w // 2, 0)).astype(F32)
        p = win / cnt - ag[POOL_HALO:POOL_HALO + ts]
        ys.append(_dot(p.astype(BF16), pw_ref[gi]))
    y = (jnp.concatenate(ys, axis=1) + pb_ref[...]) * pls_ref[...]
    _post_mixer(x, y, g1[...], n2g[...], sh2[...], sc2[...], wrh_ref, wrl_ref, br_ref,
                x1_ref, h2_ref, gates_ref)


def _pool_layer(x, m, row_fn, n1g, n2g, pw, pb, pls, router):
    b, s, _ = x.shape
    ts = min(512, s)
    nh = s // POOL_HALO
    r = ts // POOL_HALO
    shapes, specs = _post_out((b, s), ts)
    rf = lambda bi, i: row_fn(bi)
    return pl.pallas_call(
        functools.partial(_pool_kernel, seq=s, ts=ts),
        out_shape=shapes,
        grid=(b, s // ts),
        in_specs=[
            pl.BlockSpec((None, POOL_HALO, D), lambda bi, i: (bi, jnp.maximum(i * r - 1, 0), 0)),
            pl.BlockSpec((None, ts, D), lambda bi, i: (bi, i, 0)),
            pl.BlockSpec((None, POOL_HALO, D), lambda bi, i: (bi, jnp.minimum((i + 1) * r, nh - 1), 0)),
            _mod_spec(0, rf), _mod_spec(1, rf), _mod_spec(2, rf), _mod_spec(3, rf), _mod_spec(4, rf),
            _full_spec((1, D)), _full_spec((1, D)),
            _full_spec(pw.shape), _full_spec((1, D)), _full_spec((1, D)),
        ] + _router_specs(),
        out_specs=specs,
        compiler_params=pltpu.CompilerParams(dimension_semantics=("arbitrary", "arbitrary"),
                                             vmem_limit_bytes=VMEM_LIMIT),
        name="pool_layer",
    )(x, x, x, m, m, m, m, m, n1g, n2g, pw, pb, pls, *router)


def _run_copies(src, dst, src_off, dst_off, n, sem, wait, fixed_src=False):
    done = jnp.int32(0)
    for bit in RUN_BITS:
        take = (n & bit) != 0

        @pl.when(take)
        def _(done=done, bit=bit):
            so = pl.multiple_of(src_off + (0 if fixed_src else done), RUN_ALIGN)
            do = pl.multiple_of(dst_off + done, RUN_ALIGN)
            cp = pltpu.make_async_copy(src.at[pl.ds(so, bit)], dst.at[pl.ds(do, bit)], sem)
            if wait:
                cp.wait()
            else:
                cp.start()

        done = done + jnp.where(take, bit, 0)


def _dispatch_kernel(h_ref, r_ref, hs_ref, gs_ref, slot_ref, tab_ref, wtab_ref,
                     hbuf, gbuf, zh, zg, sems, cnt, pend, *, ntiles, nsub, rcap, nw):
    i = pl.program_id(0)
    nsteps = ntiles // nsub
    sl = lax.rem(i, 2)
    t = SORT_T

    @pl.when(i == 0)
    def _():
        for g in range(N_GROUPS):
            cnt[g] = 0
        zh[...] = jnp.zeros_like(zh)
        zg[...] = jnp.zeros_like(zg)

    def runs(slot_idx, wait):
        for g in range(N_GROUPS):
            off, n, dst = pend[slot_idx, g], pend[slot_idx, N_GROUPS + g], pend[slot_idx, 2 * N_GROUPS + g]
            _run_copies(hbuf.at[slot_idx], hs_ref, off, dst, n, sems.at[slot_idx], wait)
            _run_copies(gbuf.at[slot_idx], gs_ref, off, dst, n, sems.at[slot_idx], wait)

    @pl.when(i >= 2)
    def _():
        for sub in range(nsub):
            runs(sl * nsub + sub, True)

    lane = lax.broadcasted_iota(jnp.int32, (t, LANES), 1)
    lane_f = lane.astype(F32)
    before = lax.broadcasted_iota(jnp.int32, (t, t), 1) < lax.broadcasted_iota(jnp.int32, (t, t), 0)
    before = jnp.where(before, 1.0, 0.0).astype(BF16)
    srow = lax.broadcasted_iota(jnp.int32, (SORT_TS, t), 0).astype(F32)
    for sub in range(nsub):
        bi, tile, rows = sl * nsub + sub, i * nsub + sub, pl.ds(sub * t, t)
        r = r_ref[rows, :]
        onehot = jnp.where(lane_f == r[:, GROUP_LANE:GROUP_LANE + 1], 1.0, 0.0)
        rank = _dot(before, onehot.astype(BF16))
        counts = jnp.sum(onehot, axis=0, keepdims=True)
        off = jnp.int32(0)
        off_lane = jnp.zeros((1, LANES), F32)
        for g in range(N_GROUPS):
            c = counts[0, g].astype(jnp.int32)
            cpad = ((c + (RUN_ALIGN - 1)) // RUN_ALIGN) * RUN_ALIGN
            pend[bi, g] = off
            pend[bi, N_GROUPS + g] = cpad
            pend[bi, 2 * N_GROUPS + g] = g * rcap + cnt[g]
            tab_ref[tile, g] = cnt[g]
            tab_ref[tile, N_GROUPS + g] = cpad
            cnt[g] = cnt[g] + cpad
            off_lane = jnp.where(lane[:1] == g, off.astype(F32), off_lane)
            off = off + cpad
        slot = jnp.sum(onehot * (rank + off_lane), axis=-1, keepdims=True)
        slot_b = jnp.broadcast_to(slot, (t, LANES))
        slot_ref[rows, :] = slot_b
        slot_row = slot_b.T[:1]
        perm = jnp.where(srow == slot_row, 1.0, 0.0).astype(BF16)
        hbuf[bi] = _dot(perm, h_ref[rows, :]).astype(BF16)
        r_hi = r.astype(BF16).astype(F32)
        r_mid = (r - r_hi).astype(BF16).astype(F32)
        r_lo = r - r_hi - r_mid
        pieces = r_hi + pltpu.roll(r_mid, GATE_PIECE, 1) + pltpu.roll(r_lo, 2 * GATE_PIECE, 1)
        gbuf[bi] = _dot(perm, pieces.astype(BF16))
    for sub in range(nsub):
        runs(sl * nsub + sub, False)

    @pl.when(i == nsteps - 1)
    def _():
        if nsteps >= 2:
            for sub in range(nsub):
                runs((1 - sl) * nsub + sub, True)
        for sub in range(nsub):
            runs(sl * nsub + sub, True)
        blocks_per_region = rcap // MOE_TM
        starts, total = [], jnp.int32(0)
        for wait in (False, True):
            for g in range(N_GROUPS):
                c = cnt[g]
                nt = (c + (MOE_TM - 1)) // MOE_TM
                for buf, dst in ((zh, hs_ref), (zg, gs_ref)):
                    _run_copies(buf, dst, 0, g * rcap + c, nt * MOE_TM - c, sems.at[0], wait, fixed_src=True)
                if not wait:
                    starts.append(total)
                    total = total + nt

        def item(w, carry):
            wc = jnp.minimum(w, total - 1)
            grp = sum((wc >= starts[g]).astype(jnp.int32) for g in range(1, N_GROUPS))
            first = jnp.int32(0)
            for g in range(1, N_GROUPS):
                first = jnp.where(grp == g, starts[g], first)
            blk = grp * blocks_per_region + (wc - first)
            valid = (w < total).astype(jnp.int32)
            wtab_ref[0, w] = blk
            wtab_ref[1, w] = jnp.where(valid == 1, blk, N_GROUPS * blocks_per_region)
            wtab_ref[2, w] = grp
            wtab_ref[3, w] = valid
            return carry

        lax.fori_loop(0, nw, item, 0)


def _moe_kernel(wtab, hs_ref, gs_ref, wg_ref, wu_ref, wd_ref, ys_ref):
    w = pl.program_id(0)

    @pl.when(wtab[3, w] == 1)
    def _():
        h = hs_ref[...]
        gates = gs_ref[...]
        parts = []
        for e in range(EPG):
            a = _dot(h, wg_ref[e])
            u = _dot(h, wu_ref[e])
            gate = sum(gates[:, k * GATE_PIECE + e:k * GATE_PIECE + e + 1] for k in range(3))
            parts.append((_silu(a) * u * gate).astype(BF16))
        hid = jnp.concatenate(parts, axis=1)
        ys_ref[...] = _dot(hid, wd_ref[...].reshape(EPG * D_EXPERT, D)).astype(BF16)

    @pl.when(wtab[3, w] == 0)
    def _():
        ys_ref[...] = jnp.zeros_like(ys_ref)


def _combine_kernel(tab, x1_ref, slot_ref, g2_ref, ys_ref, o_ref, ybuf, sems, *, ntiles, nsub, rcap):
    i = pl.program_id(0)
    nsteps = ntiles // nsub
    sl = lax.rem(i, 2)

    def gather(step, parity, wait):
        for sub in range(nsub):
            tile, bi = step * nsub + sub, parity * nsub + sub
            off = jnp.int32(0)
            for g in range(N_GROUPS):
                n = tab[tile, N_GROUPS + g]
                _run_copies(ys_ref, ybuf.at[bi], g * rcap + tab[tile, g], off, n, sems.at[bi], wait)
                off = off + n

    @pl.when(i == 0)
    def _():
        ybuf[...] = jnp.zeros_like(ybuf)
        gather(0, 0, False)

    @pl.when(i + 1 < nsteps)
    def _():
        gather(i + 1, 1 - sl, False)

    gather(i, sl, True)
    lane_s = lax.broadcasted_iota(jnp.int32, (SORT_T, SORT_TS), 1).astype(F32)
    for sub in range(nsub):
        rows = pl.ds(sub * SORT_T, SORT_T)
        unperm = jnp.where(lane_s == slot_ref[rows, :1], 1.0, 0.0).astype(BF16)
        o_ref[rows, :] = x1_ref[rows, :] + g2_ref[...] * _dot(unperm, ybuf[sl * nsub + sub])


def _moe_layer(h2, x1, route, m, row_fn, wg, wu, wd):
    b, s, _ = x1.shape
    n = b * s
    assert n % SORT_T == 0 and n % MOE_TM == 0
    ntiles = n // SORT_T
    nsub = 2 if ntiles % 2 == 0 and s % (2 * SORT_T) == 0 else 1
    step = nsub * SORT_T
    rcap = n
    rtot = N_GROUPS * rcap + MOE_TM
    nw = -(-ntiles * SORT_TS // MOE_TM) + N_GROUPS
    arb = pltpu.CompilerParams(dimension_semantics=("arbitrary",), vmem_limit_bytes=VMEM_LIMIT)
    any_spec = pl.BlockSpec(memory_space=pl.ANY)
    smem_spec = pl.BlockSpec(memory_space=pltpu.SMEM)
    hs, gs, slot, tab, wtab = pl.pallas_call(
        functools.partial(_dispatch_kernel, ntiles=ntiles, nsub=nsub, rcap=rcap, nw=nw),
        out_shape=(
            jax.ShapeDtypeStruct((rtot, D), BF16),
            jax.ShapeDtypeStruct((rtot, LANES), F32),
            jax.ShapeDtypeStruct((n, LANES), F32),
            jax.ShapeDtypeStruct((ntiles, 2 * N_GROUPS), jnp.int32),
            jax.ShapeDtypeStruct((4, nw), jnp.int32),
        ),
        grid=(ntiles // nsub,),
        in_specs=[pl.BlockSpec((step, D), lambda i: (i, 0)), pl.BlockSpec((step, LANES), lambda i: (i, 0))],
        out_specs=(any_spec, any_spec, pl.BlockSpec((step, LANES), lambda i: (i, 0)), smem_spec, smem_spec),
        scratch_shapes=[
            pltpu.VMEM((2 * nsub, SORT_TS, D), BF16),
            pltpu.VMEM((2 * nsub, SORT_TS, LANES), F32),
            pltpu.VMEM((MOE_TM // 2, D), BF16),
            pltpu.VMEM((MOE_TM // 2, LANES), F32),
            pltpu.SemaphoreType.DMA((2 * nsub,)),
            pltpu.SMEM((N_GROUPS,), jnp.int32),
            pltpu.SMEM((2 * nsub, 3 * N_GROUPS), jnp.int32),
        ],
        compiler_params=arb,
        name="moe_dispatch",
    )(h2.reshape(n, D), route.reshape(n, LANES))

    ys = pl.pallas_call(
        _moe_kernel,
        out_shape=jax.ShapeDtypeStruct((rtot, D), BF16),
        grid_spec=pltpu.PrefetchScalarGridSpec(
            num_scalar_prefetch=1,
            grid=(nw,),
            in_specs=[
                pl.BlockSpec((MOE_TM, D), lambda w, tb: (tb[0, w], 0)),
                pl.BlockSpec((MOE_TM, LANES), lambda w, tb: (tb[0, w], 0)),
                pl.BlockSpec((EPG, D, D_EXPERT), lambda w, tb: (tb[2, w], 0, 0)),
                pl.BlockSpec((EPG, D, D_EXPERT), lambda w, tb: (tb[2, w], 0, 0)),
                pl.BlockSpec((EPG, D_EXPERT, D), lambda w, tb: (tb[2, w], 0, 0)),
            ],
            out_specs=pl.BlockSpec((MOE_TM, D), lambda w, tb: (tb[1, w], 0)),
        ),
        compiler_params=arb,
        name="moe_experts",
    )(wtab, hs, gs, wg, wu, wd)

    out = pl.pallas_call(
        functools.partial(_combine_kernel, ntiles=ntiles, nsub=nsub, rcap=rcap),
        out_shape=jax.ShapeDtypeStruct((n, D), F32),
        grid_spec=pltpu.PrefetchScalarGridSpec(
            num_scalar_prefetch=1,
            grid=(ntiles // nsub,),
            in_specs=[
                pl.BlockSpec((step, D), lambda i, tb: (i, 0)),
                pl.BlockSpec((step, LANES), lambda i, tb: (i, 0)),
                _mod_spec(5, lambda i, tb: row_fn(i * step // s)),
                any_spec,
            ],
            out_specs=pl.BlockSpec((step, D), lambda i, tb: (i, 0)),
            scratch_shapes=[pltpu.VMEM((2 * nsub, SORT_TS, D), BF16), pltpu.SemaphoreType.DMA((2 * nsub,))],
        ),
        compiler_params=arb,
        name="moe_combine",
    )(tab, x1.reshape(n, D), slot, m, ys)
    return out.reshape(b, s, D)


def _qkv_kernel(x_ref, sh1, sc1, n1g, w_ref, qg_ref, kg_ref, cos_ref, sin_ref, *out_refs, with_q):
    a = _rms_mod(x_ref[...], n1g[...], sh1[...], sc1[...]).astype(BF16)
    qkv = _dot(a, w_ref[...])
    cos, sin = cos_ref[...], sin_ref[...]
    lane = lax.broadcasted_iota(jnp.int32, cos.shape, 1)
    first_head = lane < HEAD_DIM
    first_half = (lane & 31) < 16

    def norm_rope(t, g):
        sq = t * t
        s_all = jnp.sum(sq, axis=-1, keepdims=True)
        s_lo = jnp.sum(jnp.where(first_head, sq, 0.0), axis=-1, keepdims=True)
        ms = jnp.where(first_head, s_lo, s_all - s_lo) * (1.0 / HEAD_DIM)
        tn = t * lax.rsqrt(ms + EPS) * g
        up = pltpu.roll(tn, LANES - 16, 1)
        dn = pltpu.roll(tn, 16, 1)
        return tn * cos + jnp.where(first_half, up, dn) * sin

    if with_q:
        q_ref, k_ref, v_ref = out_refs
        qg = qg_ref[...]
        for j in range(D // LANES):
            blk = norm_rope(qkv[:, j * LANES:(j + 1) * LANES], qg) * Q_SCALE
            q_ref[:, j * LANES:(j + 1) * LANES] = blk.astype(BF16)
        off = D
    else:
        k_ref, v_ref = out_refs
        off = 0
    kg = kg_ref[...]
    for j in range(KV_DIM // LANES):
        kt = norm_rope(qkv[:, off + j * LANES:off + (j + 1) * LANES], kg).T.astype(BF16)
        k_ref[2 * j] = kt[:HEAD_DIM]
        k_ref[2 * j + 1] = kt[HEAD_DIM:]
        vv = qkv[:, off + KV_DIM + j * LANES:off + KV_DIM + (j + 1) * LANES]
        ones_col = jnp.where(lane == HEAD_DIM, 1.0, 0.0)
        v_ref[2 * j] = jnp.where(first_head, vv, ones_col).astype(BF16)
        v_ref[2 * j + 1] = jnp.where(first_head, pltpu.roll(vv, HEAD_DIM, 1), ones_col).astype(BF16)


def _qkv_layer(x, m, row_fn, n1g, w, qg, kg, cos_t, sin_t, with_q):
    b, s, _ = x.shape
    ts = min(512, s)
    rf = lambda bi, i: row_fn(bi)
    shapes = [jax.ShapeDtypeStruct((b, N_KV, HEAD_DIM, s), BF16), jax.ShapeDtypeStruct((b, N_KV, s, LANES), BF16)]
    specs = [pl.BlockSpec((None, N_KV, HEAD_DIM, ts), lambda bi, i: (bi, 0, 0, i)),
             pl.BlockSpec((None, N_KV, ts, LANES), lambda bi, i: (bi, 0, i, 0))]
    if with_q:
        shapes = [jax.ShapeDtypeStruct((b, s, D), BF16)] + shapes
        specs = [pl.BlockSpec((None, ts, D), lambda bi, i: (bi, i, 0))] + specs
    return pl.pallas_call(
        functools.partial(_qkv_kernel, with_q=with_q),
        out_shape=tuple(shapes),
        grid=(b, s // ts),
        in_specs=[
            pl.BlockSpec((None, ts, D), lambda bi, i: (bi, i, 0)),
            _mod_spec(0, rf), _mod_spec(1, rf),
            _full_spec((1, D)),
            _full_spec(w.shape),
            _full_spec((1, LANES)), _full_spec((1, LANES)),
            pl.BlockSpec((ts, LANES), lambda bi, i: (i, 0)),
            pl.BlockSpec((ts, LANES), lambda bi, i: (i, 0)),
        ],
        out_specs=tuple(specs),
        compiler_params=pltpu.CompilerParams(dimension_semantics=("arbitrary", "arbitrary"),
                                             vmem_limit_bytes=VMEM_LIMIT),
        name="qkv_layer",
    )(x, m, m, n1g, w, qg, kg, cos_t, sin_t)


def _attn_kernel(q_ref, kt_ref, v_ref, o_ref, *, tq, tc):
    kt, v = kt_ref[...], v_ref[...]
    for r in range(tq // tc):
        for j in range(Q_PER_KV):
            rows, cols = pl.ds(r * tc, tc), pl.ds(j * HEAD_DIM, HEAD_DIM)
            s = _dot(q_ref[rows, cols], kt)
            m = jnp.max(s, axis=-1, keepdims=True)
            p = jnp.exp2(s - m)
            o = _dot(p.astype(BF16), v)
            o = o[:, :HEAD_DIM] / o[:, HEAD_DIM:HEAD_DIM + 1]
            o_ref[rows, cols] = o.astype(BF16)


def _attention(q, kt_all, v_all):
    b, s, _ = q.shape
    nk = v_all.shape[2]
    tq, tc = 512, 256
    grp = Q_PER_KV * HEAD_DIM
    return pl.pallas_call(
        functools.partial(_attn_kernel, tq=tq, tc=tc),
        out_shape=jax.ShapeDtypeStruct((b, s, D), BF16),
        grid=(b, N_KV, s // tq),
        in_specs=[
            pl.BlockSpec((None, tq, grp), lambda bi, h, i: (bi, i, h)),
            pl.BlockSpec((None, None, HEAD_DIM, nk), lambda bi, h, i: (bi, h, 0, 0)),
            pl.BlockSpec((None, None, nk, LANES), lambda bi, h, i: (bi, h, 0, 0)),
        ],
        out_specs=pl.BlockSpec((None, tq, grp), lambda bi, h, i: (bi, i, h)),
        compiler_params=pltpu.CompilerParams(dimension_semantics=("arbitrary", "arbitrary", "arbitrary"),
                                             vmem_limit_bytes=VMEM_LIMIT),
        name="attention",
    )(q, kt_all, v_all)


def _proj_kernel(o_ref, x_ref, g1, sh2, sc2, n2g, wo_ref, wrh_ref, wrl_ref, br_ref,
                 x1_ref, h2_ref, gates_ref):
    y = _dot(o_ref[...], wo_ref[...])
    _post_mixer(x_ref[...], y, g1[...], n2g[...], sh2[...], sc2[...], wrh_ref, wrl_ref, br_ref,
                x1_ref, h2_ref, gates_ref)


def _proj_layer(o, x, m, row_fn, n2g, wo, router):
    b, s, _ = x.shape
    ts = min(512, s)
    rf = lambda bi, i: row_fn(bi)
    shapes, specs = _post_out((b, s), ts)
    return pl.pallas_call(
        _proj_kernel,
        out_shape=shapes,
        grid=(b, s // ts),
        in_specs=[
            pl.BlockSpec((None, ts, D), lambda bi, i: (bi, i, 0)),
            pl.BlockSpec((None, ts, D), lambda bi, i: (bi, i, 0)),
            _mod_spec(2, rf), _mod_spec(3, rf), _mod_spec(4, rf),
            _full_spec((1, D)),
            _full_spec((D, D)),
        ] + _router_specs(),
        out_specs=specs,
        compiler_params=pltpu.CompilerParams(dimension_semantics=("arbitrary", "arbitrary"),
                                             vmem_limit_bytes=VMEM_LIMIT),
        name="attn_proj",
    )(o, x, m, m, m, n2g, wo, *router)


def _gelu_tanh(x):
    c = 0.7978845608028654
    return x * (0.5 * (1.0 + jnp.tanh(c * (x + 0.044715 * (x * x * x)))))


def _sgu_kernel(x_ref, sh1, sc1, g1, sh2, sc2, n1g, n2g, win_ref, bin_ref, lng_ref, lnb_ref, ws_ref,
                bs_ref, wout_ref, wrh_ref, wrl_ref, br_ref, x1_ref, h2_ref, gates_ref, *, ts):
    x = x_ref[...]
    a = _rms_mod(x, n1g[...], sh1[...], sc1[...]).astype(BF16)
    z = _gelu_tanh(_dot(a, win_ref[...]) + bin_ref[...])
    u, v = z[:, :D], z[:, D:]
    mu = jnp.mean(v, axis=-1, keepdims=True)
    vc = v - mu
    var = jnp.mean(vc * vc, axis=-1, keepdims=True)
    vb = (vc * lax.rsqrt(var + EPS) * lng_ref[...] + lnb_ref[...]).astype(BF16)
    bias = bs_ref[...]
    rows = []
    for c in range(ts // CHUNK):
        cols = [_dot(ws_ref[g], vb[c * CHUNK:(c + 1) * CHUNK, g * LANES:(g + 1) * LANES])
                for g in range(SGU_G)]
        rows.append(jnp.concatenate(cols, axis=1) + bias)
    sv = jnp.concatenate(rows, axis=0)
    y = _dot((u * sv).astype(BF16), wout_ref[...])
    _post_mixer(x, y, g1[...], n2g[...], sh2[...], sc2[...], wrh_ref, wrl_ref, br_ref,
                x1_ref, h2_ref, gates_ref)


def _sgu_layer(x, m, row_fn, n1g, n2g, win, b_in, lng, lnb, ws, bs_full, wout, router):
    b, s, _ = x.shape
    ts = min(512, s)
    rf = lambda bi, i: row_fn(bi)
    shapes, specs = _post_out((b, s), ts)
    return pl.pallas_call(
        functools.partial(_sgu_kernel, ts=ts),
        out_shape=shapes,
        grid=(b, s // ts),
        in_specs=[
            pl.BlockSpec((None, ts, D), lambda bi, i: (bi, i, 0)),
            _mod_spec(0, rf), _mod_spec(1, rf), _mod_spec(2, rf), _mod_spec(3, rf), _mod_spec(4, rf),
            _full_spec((1, D)), _full_spec((1, D)),
            _full_spec((D, 2 * D)), _full_spec((1, 2 * D)),
            _full_spec((1, D)), _full_spec((1, D)),
            _full_spec((SGU_G, CHUNK, CHUNK)), _full_spec((CHUNK, D)),
            _full_spec((D, D)),
        ] + _router_specs(),
        out_specs=specs,
        compiler_params=pltpu.CompilerParams(dimension_semantics=("arbitrary", "arbitrary"),
                                             vmem_limit_bytes=VMEM_LIMIT),
        name="sgu_layer",
    )(x, m, m, m, m, m, n1g, n2g, win, b_in, lng, lnb, ws, bs_full, wout, *router)


def _rope_tables(s):
    t = jnp.arange(s)
    pos = jnp.stack([t // GRID_W, t % GRID_W], axis=-1).astype(F32)
    inv = ROPE_THETA ** (-jnp.arange(0, HEAD_DIM // 2, 2, dtype=F32) / (HEAD_DIM // 2))
    lane = jnp.arange(LANES)
    d = lane % HEAD_DIM
    ang = pos[:, d // 32] * inv[d % 16][None, :]
    sign = jnp.where((d % 32) < 16, -1.0, 1.0).astype(F32)
    return jnp.cos(ang), jnp.sin(ang) * sign[None, :]


def _router_params(w_rg, b_rg, w_re, b_re):
    w = jnp.concatenate([w_rg, w_re], axis=1)
    w = jnp.pad(w, ((0, 0), (0, LANES - w.shape[1])))
    hi = w.astype(BF16)
    lo = (w - hi.astype(F32)).astype(BF16)
    bias = jnp.concatenate([b_rg, b_re])
    bias = jnp.pad(bias, (0, LANES - bias.shape[0])).reshape(1, LANES)
    return hi, lo, bias


def kernel(x, c, ctx, c_ctx, w_mod, b_mod, norm1_g, norm2_g, pool_w, pool_b, pool_ls, attn_wqkv, attn_qg,
           attn_kg, attn_wo, sgu_win, sgu_bin, sgu_lng, sgu_lnb, sgu_ws, sgu_bs, sgu_wout, moe_wrg, moe_brg,
           moe_wre, moe_bre, moe_wg, moe_wu, moe_wd):
    b, s, _ = x.shape
    depth = w_mod.shape[0]
    assert b <= CTX_ROW and x.shape[2] == D and s % GRID_W == 0
    attn_layers = [i for i in range(depth) if i % N_MIX == 1]
    ctx_last = max(attn_layers, default=-1)

    cc = jnp.zeros((MOD_ROWS, D), F32).at[:b].set(c).at[CTX_ROW].set(c_ctx)
    mods = _adaln(cc, w_mod, b_mod).reshape(depth, MOD_ROWS, N_MOD, 1, D)
    lat_row = lambda bi: bi
    ctx_row = lambda bi: CTX_ROW
    cos_t, sin_t = _rope_tables(s)
    ones_t = jnp.ones((ctx.shape[1], LANES), F32)
    zeros_t = jnp.zeros((ctx.shape[1], LANES), F32)

    sctx = ctx
    for i in range(depth):
        kind, j = i % N_MIX, i // N_MIX
        ctx_in, ctx_upd = i <= ctx_last, i < ctx_last
        m = mods[i]
        n1g, n2g = norm1_g[i].reshape(1, D), norm2_g[i].reshape(1, D)
        router = _router_params(moe_wrg[i], moe_brg[i], moe_wre[i], moe_bre[i])
        wg, wu, wd = moe_wg[i].astype(BF16), moe_wu[i].astype(BF16), moe_wd[i].astype(BF16)
        streams = [(x, lat_row)] + ([(sctx, ctx_row)] if ctx_upd else [])
        outs = []
        if kind == 0:
            pw, pb = pool_w[j].astype(BF16), pool_b[j].reshape(1, D)
            pls = pool_ls[j].reshape(1, D)
            for xs, rf in streams:
                outs.append(_pool_layer(xs, m, rf, n1g, n2g, pw, pb, pls, router))
        elif kind == 1:
            w = attn_wqkv[j].astype(BF16)
            qg = jnp.tile(attn_qg[j], LANES // HEAD_DIM).reshape(1, LANES)
            kg = jnp.tile(attn_kg[j], LANES // HEAD_DIM).reshape(1, LANES)
            q, k, v = _qkv_layer(x, m, lat_row, n1g, w, qg, kg, cos_t, sin_t, True)
            k_c, v_c = _qkv_layer(sctx, m, ctx_row, n1g, w[:, D:], qg, kg, ones_t, zeros_t, False)
            o = _attention(q, jnp.concatenate([k_c, k], axis=3), jnp.concatenate([v_c, v], axis=2))
            outs.append(_proj_layer(o, x, m, lat_row, n2g, attn_wo[j].astype(BF16), router))
            assert not ctx_upd
        else:
            bs_full = jnp.repeat(sgu_bs[j].T, LANES, axis=1)
            args = (sgu_win[j].astype(BF16), sgu_bin[j].reshape(1, 2 * D), sgu_lng[j].reshape(1, D),
                    sgu_lnb[j].reshape(1, D), sgu_ws[j].astype(BF16), bs_full, sgu_wout[j].astype(BF16))
            for xs, rf in streams:
                outs.append(_sgu_layer(xs, m, rf, n1g, n2g, *args, router))
        new = [_moe_layer(h2, x1, gates, m, rf, wg, wu, wd)
               for (x1, h2, gates), (_, rf) in zip(outs, streams)]
        x = new[0]
        if ctx_upd:
            sctx = new[1]
    return x
```

```python
import functools
from typing import NamedTuple

import jax
import jax.numpy as jnp
from jax import lax
from jax.experimental import pallas as pl
from jax.experimental.pallas import tpu as pltpu

F32 = jnp.float32
BF16 = jnp.bfloat16

D = 1024
GRID_W = 64
EPS = 1e-6
N_MIX = 3
POOL_WINDOWS = (2, 4, 8, 16)
POOL_CH = D // len(POOL_WINDOWS)
POOL_HALO = 8
HEAD_DIM = 64
N_KV = 4
Q_PER_KV = 4
KV_DIM = N_KV * HEAD_DIM
ROPE_THETA = 10000.0
Q_SCALE = HEAD_DIM ** -0.5 * 1.4426950408889634
CHUNK = 128
SGU_G = 8
N_GROUPS = 4
EPG = 4
D_EXPERT = 256
N_MOD = 6
LANES = 128
GROUP_LANE = 4
GATE_PIECE = 8
SORT_T = 256
RUN_ALIGN = 16
RUN_PAD = N_GROUPS * RUN_ALIGN
MOE_TM = 512
WTAB_ROWS = 5
CTX_ROW = 8
MOD_ROWS = 16
VMEM_LIMIT = 56 * 1024 * 1024


def _dot(a, b):
    return jnp.dot(a, b, preferred_element_type=F32)


def _rms_mod(x, g, sh, sc):
    ms = jnp.mean(x * x, axis=-1, keepdims=True)
    return x * lax.rsqrt(ms + EPS) * g * (1.0 + sc) + sh


def _silu(x):
    return x * (1.0 / (1.0 + jnp.exp(-x)))


def _route(h, wrh_ref, wrl_ref, br_ref):
    h_hi = h.astype(BF16)
    h_lo = (h - h_hi.astype(F32)).astype(BF16)
    whi = wrh_ref[...]
    lg = _dot(h_hi, whi) + _dot(h_hi, wrl_ref[...]) + _dot(h_lo, whi) + br_ref[...]
    lane = lax.broadcasted_iota(jnp.int32, lg.shape, 1)
    lane_f = lane.astype(F32)
    neg = -jnp.inf
    gmask = lane < N_GROUPS
    gl = jnp.where(gmask, lg, neg)
    gmax = jnp.max(gl, axis=-1, keepdims=True)
    gsel = jnp.min(jnp.where(gl == gmax, lane_f, float(LANES)), axis=-1, keepdims=True)
    gsum = jnp.sum(jnp.exp(gl - gmax), axis=-1, keepdims=True)
    g_w = 1.0 / gsum
    lo_lane = N_GROUPS + EPG * gsel
    emask = (lane_f >= lo_lane) & (lane_f < lo_lane + EPG)
    el = jnp.where(emask, lg, neg)
    t1 = jnp.max(el, axis=-1, keepdims=True)
    i1 = jnp.min(jnp.where(el == t1, lane_f, float(LANES)), axis=-1, keepdims=True)
    el2 = jnp.where(lane_f == i1, neg, el)
    t2 = jnp.max(el2, axis=-1, keepdims=True)
    i2 = jnp.min(jnp.where(el2 == t2, lane_f, float(LANES)), axis=-1, keepdims=True)
    e = jnp.exp(t2 - t1)
    w1 = 1.0 / (1.0 + e)
    w2 = e / (1.0 + e)
    rec = jnp.where(lane_f == i1 - lo_lane, w1 * g_w, jnp.where(lane_f == i2 - lo_lane, w2 * g_w, 0.0))
    return jnp.where(lane == GROUP_LANE, gsel, rec)


def _post_mixer(x, y, g1, n2g, sh2, sc2, wrh_ref, wrl_ref, br_ref, x1_ref, d, step):
    x1 = x + g1 * y
    h2 = _rms_mod(x1, n2g, sh2, sc2)
    x1_ref[...] = x1
    rec = _route(h2, wrh_ref, wrl_ref, br_ref)
    hb = h2.astype(BF16)
    consts = _dispatch_consts(d.plan)
    for sub in range(d.plan.nsub):
        rows = slice(sub * d.plan.t, (sub + 1) * d.plan.t)
        _dispatch_sort(d, step, sub, rec[rows], hb[rows], consts)
    _dispatch_send(d, step)


def _adaln_kernel(cc_ref, w_ref, b_ref, o_ref):
    s = _silu(cc_ref[...]).astype(BF16)
    o_ref[...] = _dot(s, w_ref[...].astype(BF16)) + b_ref[...]


def _adaln(cc, w_mod, b_mod):
    depth = w_mod.shape[0]
    return pl.pallas_call(
        _adaln_kernel,
        out_shape=jax.ShapeDtypeStruct((depth, MOD_ROWS, N_MOD * D), F32),
        grid=(depth, N_MOD),
        in_specs=[
            pl.BlockSpec((MOD_ROWS, D), lambda i, k: (0, 0)),
            pl.BlockSpec((None, D, D), lambda i, k: (i, 0, k)),
            pl.BlockSpec((None, 1, D), lambda i, k: (i, 0, k)),
        ],
        out_specs=pl.BlockSpec((None, MOD_ROWS, D), lambda i, k: (i, 0, k)),
        compiler_params=pltpu.CompilerParams(dimension_semantics=("arbitrary", "arbitrary")),
        name="adaln",
    )(cc, w_mod, b_mod.reshape(depth, 1, N_MOD * D))


def _mod_spec(k, row_fn):
    return pl.BlockSpec((None, None, 1, D), lambda *idx: (row_fn(*idx), k, 0, 0))


def _full_spec(shape):
    nd = len(shape)
    return pl.BlockSpec(shape, lambda *idx: (0,) * nd)


def _router_specs():
    return [_full_spec((D, LANES)), _full_spec((D, LANES)), _full_spec((1, LANES))]


class _Plan(NamedTuple):
    t: int
    ts: int
    nsub: int
    nsteps: int
    ntiles: int
    rcap: int
    rtot: int
    nw: int
    bits: tuple


def _make_plan(b, s, tile_rows):
    n = b * s
    t = min(SORT_T, tile_rows)
    assert n % MOE_TM == 0 and tile_rows % t == 0 and t % RUN_ALIGN == 0
    ntiles = n // t
    bits = tuple(t >> k for k in range(t.bit_length()) if (t >> k) >= RUN_ALIGN)
    return _Plan(t=t, ts=t + RUN_PAD, nsub=tile_rows // t, nsteps=n // tile_rows, ntiles=ntiles, rcap=n,
                 rtot=N_GROUPS * n + MOE_TM, nw=-(-ntiles * (t + RUN_PAD) // MOE_TM) + N_GROUPS, bits=bits)


class _Dispatch:
    def __init__(self, plan, refs):
        self.plan = plan
        (self.hs, self.gs, self.slot, self.tab, self.wtab,
         self.hbuf, self.gbuf, self.zh, self.zg, self.sems, self.cnt, self.pend) = refs


def _mixer_out(b, s, ts, plan):
    tile = lambda bi, i: (bi, i, 0)
    any_spec = pl.BlockSpec(memory_space=pl.ANY)
    smem_spec = pl.BlockSpec(memory_space=pltpu.SMEM)
    shapes = (
        jax.ShapeDtypeStruct((b, s, D), F32),
        jax.ShapeDtypeStruct((plan.rtot, D), BF16),
        jax.ShapeDtypeStruct((plan.rtot, LANES), F32),
        jax.ShapeDtypeStruct((b, s, LANES), F32),
        jax.ShapeDtypeStruct((plan.ntiles, 2 * N_GROUPS), jnp.int32),
        jax.ShapeDtypeStruct((WTAB_ROWS, plan.nw), jnp.int32),
    )
    specs = (pl.BlockSpec((None, ts, D), tile), any_spec, any_spec, pl.BlockSpec((None, ts, LANES), tile),
             smem_spec, smem_spec)
    scratch = [
        pltpu.VMEM((2 * plan.nsub, plan.ts, D), BF16),
        pltpu.VMEM((2 * plan.nsub, plan.ts, LANES), F32),
        pltpu.VMEM((MOE_TM // 2, D), BF16),
        pltpu.VMEM((MOE_TM // 2, LANES), F32),
        pltpu.SemaphoreType.DMA((2 * plan.nsub,)),
        pltpu.SMEM((N_GROUPS,), jnp.int32),
        pltpu.SMEM((2 * plan.nsub, 3 * N_GROUPS), jnp.int32),
    ]
    return shapes, specs, scratch


def _grid_step():
    return pl.program_id(0) * pl.num_programs(1) + pl.program_id(1)


def _pool_kernel(xp_ref, xc_ref, xn_ref, sh1, sc1, g1, sh2, sc2, n1g, n2g, pw_ref, pb_ref, pls_ref,
                 wrh_ref, wrl_ref, br_ref, x1_ref, *disp_refs, seq, ts, plan):
    d, step = _Dispatch(plan, disp_refs), _grid_step()
    _dispatch_begin(d, step)
    i = pl.program_id(1)
    nt = pl.num_programs(1)
    x = xc_ref[...]
    g, sh, sc = n1g[...], sh1[...], sc1[...]
    a_c = _rms_mod(x, g, sh, sc)
    a_p = _rms_mod(xp_ref[...], g, sh, sc) * (i > 0).astype(F32)
    a_n = _rms_mod(xn_ref[...], g, sh, sc) * (i < nt - 1).astype(F32)
    a_ext = jnp.concatenate([a_p, a_c, a_n], axis=0)
    n_ext = ts + 2 * POOL_HALO
    t = i * ts + lax.broadcasted_iota(jnp.int32, (ts, 1), 0)
    ys = []
    for gi, w in enumerate(POOL_WINDOWS):
        ag = a_ext[:, gi * POOL_CH:(gi + 1) * POOL_CH]
        acc = ag + pltpu.roll(ag, 1, 0)
        half = 1
        while 2 * half < w:
            acc = pltpu.roll(acc, half, 0) + pltpu.roll(acc, n_ext - half, 0)
            half *= 2
        win = acc[POOL_HALO:POOL_HALO + ts]
        cnt = (jnp.minimum(t + w // 2, seq) - jnp.maximum(t - w // 2, 0)).astype(F32)
        p = win / cnt - ag[POOL_HALO:POOL_HALO + ts]
        ys.append(_dot(p.astype(BF16), pw_ref[gi]))
    y = (jnp.concatenate(ys, axis=1) + pb_ref[...]) * pls_ref[...]
    _post_mixer(x, y, g1[...], n2g[...], sh2[...], sc2[...], wrh_ref, wrl_ref, br_ref, x1_ref, d, step)


def _pool_layer(x, m, row_fn, n1g, n2g, pw, pb, pls, router):
    b, s, _ = x.shape
    ts = min(512, s)
    nh = s // POOL_HALO
    r = ts // POOL_HALO
    plan = _make_plan(b, s, ts)
    shapes, specs, scratch = _mixer_out(b, s, ts, plan)
    rf = lambda bi, i: row_fn(bi)
    return plan, pl.pallas_call(
        functools.partial(_pool_kernel, seq=s, ts=ts, plan=plan),
        out_shape=shapes,
        scratch_shapes=scratch,
        grid=(b, s // ts),
        in_specs=[
            pl.BlockSpec((None, POOL_HALO, D), lambda bi, i: (bi, jnp.maximum(i * r - 1, 0), 0)),
            pl.BlockSpec((None, ts, D), lambda bi, i: (bi, i, 0)),
            pl.BlockSpec((None, POOL_HALO, D), lambda bi, i: (bi, jnp.minimum((i + 1) * r, nh - 1), 0)),
            _mod_spec(0, rf), _mod_spec(1, rf), _mod_spec(2, rf), _mod_spec(3, rf), _mod_spec(4, rf),
            _full_spec((1, D)), _full_spec((1, D)),
            _full_spec(pw.shape), _full_spec((1, D)), _full_spec((1, D)),
        ] + _router_specs(),
        out_specs=specs,
        compiler_params=pltpu.CompilerParams(dimension_semantics=("arbitrary", "arbitrary"),
                                             vmem_limit_bytes=VMEM_LIMIT),
        name="pool_layer",
    )(x, x, x, m, m, m, m, m, n1g, n2g, pw, pb, pls, *router)


def _run_copies(src, dst, src_off, dst_off, n, sem, wait, bits, fixed_src=False):
    done = jnp.int32(0)
    for bit in bits:
        take = (n & bit) != 0

        @pl.when(take)
        def _(done=done, bit=bit):
            so = pl.multiple_of(src_off + (0 if fixed_src else done), RUN_ALIGN)
            do = pl.multiple_of(dst_off + done, RUN_ALIGN)
            cp = pltpu.make_async_copy(src.at[pl.ds(so, bit)], dst.at[pl.ds(do, bit)], sem)
            if wait:
                cp.wait()
            else:
                cp.start()

        done = done + jnp.where(take, bit, 0)


def _dispatch_runs(d, buf, wait):
    for g in range(N_GROUPS):
        off, n, dst = d.pend[buf, g], d.pend[buf, N_GROUPS + g], d.pend[buf, 2 * N_GROUPS + g]
        _run_copies(d.hbuf.at[buf], d.hs, off, dst, n, d.sems.at[buf], wait, d.plan.bits)
        _run_copies(d.gbuf.at[buf], d.gs, off, dst, n, d.sems.at[buf], wait, d.plan.bits)


def _dispatch_begin(d, step):
    nsub = d.plan.nsub

    @pl.when(step == 0)
    def _():
        for g in range(N_GROUPS):
            d.cnt[g] = 0
        d.zh[...] = jnp.zeros_like(d.zh)
        d.zg[...] = jnp.zeros_like(d.zg)

    @pl.when(step >= 2)
    def _():
        for sub in range(nsub):
            _dispatch_runs(d, lax.rem(step, 2) * nsub + sub, True)


def _dispatch_consts(plan):
    t = plan.t
    lane = lax.broadcasted_iota(jnp.int32, (t, LANES), 1)
    before = lax.broadcasted_iota(jnp.int32, (t, t), 1) < lax.broadcasted_iota(jnp.int32, (t, t), 0)
    srow = lax.broadcasted_iota(jnp.int32, (plan.ts, t), 0).astype(F32)
    return lane, jnp.where(before, 1.0, 0.0).astype(BF16), srow


def _dispatch_sort(d, step, sub, rec, h, consts):
    plan = d.plan
    t, nsub = plan.t, plan.nsub
    lane, before, srow = consts
    buf, tile = lax.rem(step, 2) * nsub + sub, step * nsub + sub
    onehot = jnp.where(lane.astype(F32) == rec[:, GROUP_LANE:GROUP_LANE + 1], 1.0, 0.0)
    rank = _dot(before, onehot.astype(BF16))
    counts = jnp.sum(onehot, axis=0, keepdims=True)
    off = jnp.int32(0)
    off_lane = jnp.zeros((1, LANES), F32)
    for g in range(N_GROUPS):
        c = counts[0, g].astype(jnp.int32)
        cpad = ((c + (RUN_ALIGN - 1)) // RUN_ALIGN) * RUN_ALIGN
        d.pend[buf, g] = off
        d.pend[buf, N_GROUPS + g] = cpad
        d.pend[buf, 2 * N_GROUPS + g] = g * plan.rcap + d.cnt[g]
        d.tab[tile, g] = d.cnt[g]
        d.tab[tile, N_GROUPS + g] = cpad
        d.cnt[g] = d.cnt[g] + cpad
        off_lane = jnp.where(lane[:1] == g, off.astype(F32), off_lane)
        off = off + cpad
    slot = jnp.sum(onehot * (rank + off_lane), axis=-1, keepdims=True)
    slot_b = jnp.broadcast_to(slot, (t, LANES))
    d.slot[pl.ds(sub * t, t), :] = slot_b
    slot_row = slot_b.T[:1]
    perm = jnp.where(srow == slot_row, 1.0, 0.0).astype(BF16)
    d.hbuf[buf] = _dot(perm, h).astype(BF16)
    r_hi = rec.astype(BF16).astype(F32)
    r_mid = (rec - r_hi).astype(BF16).astype(F32)
    r_lo = rec - r_hi - r_mid
    pieces = r_hi + pltpu.roll(r_mid, GATE_PIECE, 1) + pltpu.roll(r_lo, 2 * GATE_PIECE, 1)
    d.gbuf[buf] = _dot(perm, pieces.astype(BF16))


def _dispatch_send(d, step):
    plan = d.plan
    nsub = plan.nsub
    sl = lax.rem(step, 2)
    for sub in range(nsub):
        _dispatch_runs(d, sl * nsub + sub, False)

    @pl.when(step == plan.nsteps - 1)
    def _():
        if plan.nsteps >= 2:
            for sub in range(nsub):
                _dispatch_runs(d, (1 - sl) * nsub + sub, True)
        for sub in range(nsub):
            _dispatch_runs(d, sl * nsub + sub, True)
        blocks_per_region = plan.rcap // MOE_TM
        tail_bits = tuple(b for b in (MOE_TM >> k for k in range(1, MOE_TM.bit_length())) if b >= RUN_ALIGN)
        starts, total = [], jnp.int32(0)
        for wait in (False, True):
            for g in range(N_GROUPS):
                c = d.cnt[g]
                nt = (c + (MOE_TM - 1)) // MOE_TM
                for zbuf, dst in ((d.zh, d.hs), (d.zg, d.gs)):
                    _run_copies(zbuf, dst, 0, g * plan.rcap + c, nt * MOE_TM - c, d.sems.at[0], wait, tail_bits,
                                fixed_src=True)
                if not wait:
                    starts.append(total)
                    total = total + nt

        def item(w, carry):
            wc = jnp.minimum(w, total - 1)
            grp = sum((wc >= starts[g]).astype(jnp.int32) for g in range(1, N_GROUPS))
            first = jnp.int32(0)
            for g in range(1, N_GROUPS):
                first = jnp.where(grp == g, starts[g], first)
            blk = grp * blocks_per_region + (wc - first)
            valid = (w < total).astype(jnp.int32)
            d.wtab[0, w] = blk
            d.wtab[1, w] = jnp.where(valid == 1, blk, N_GROUPS * blocks_per_region)
            d.wtab[2, w] = grp
            d.wtab[3, w] = valid
            d.wtab[4, w] = valid * (wc == first).astype(jnp.int32)
            return carry

        lax.fori_loop(0, plan.nw, item, 0)


def _moe_kernel(wtab, hs_ref, gs_ref, wg32_ref, wu32_ref, wd32_ref, ys_ref, wg_ref, wu_ref, wd_ref):
    w = pl.program_id(0)

    @pl.when(wtab[4, w] == 1)
    def _():
        wg_ref[...] = wg32_ref[...].astype(BF16)
        wu_ref[...] = wu32_ref[...].astype(BF16)
        wd_ref[...] = wd32_ref[...].astype(BF16)

    @pl.when(wtab[3, w] == 1)
    def _():
        h = hs_ref[...]
        gates = gs_ref[...]
        parts = []
        for e in range(EPG):
            a = _dot(h, wg_ref[e])
            u = _dot(h, wu_ref[e])
            gate = sum(gates[:, k * GATE_PIECE + e:k * GATE_PIECE + e + 1] for k in range(3))
            parts.append((_silu(a) * u * gate).astype(BF16))
        hid = jnp.concatenate(parts, axis=1)
        ys_ref[...] = _dot(hid, wd_ref[...].reshape(EPG * D_EXPERT, D)).astype(BF16)

    @pl.when(wtab[3, w] == 0)
    def _():
        ys_ref[...] = jnp.zeros_like(ys_ref)


def _combine_kernel(tab, x1_ref, slot_ref, g2_ref, ys_ref, o_ref, ybuf, sems, *, plan, nsub):
    i = pl.program_id(0)
    nsteps = plan.ntiles // nsub
    sl = lax.rem(i, 2)
    t = plan.t

    def gather(step, parity, wait):
        for sub in range(nsub):
            tile, buf = step * nsub + sub, parity * nsub + sub
            off = jnp.int32(0)
            for g in range(N_GROUPS):
                n = tab[tile, N_GROUPS + g]
                _run_copies(ys_ref, ybuf.at[buf], g * plan.rcap + tab[tile, g], off, n, sems.at[buf], wait, plan.bits)
                off = off + n

    @pl.when(i == 0)
    def _():
        ybuf[...] = jnp.zeros_like(ybuf)
        gather(0, 0, False)

    @pl.when(i + 1 < nsteps)
    def _():
        gather(i + 1, 1 - sl, False)

    gather(i, sl, True)
    lane_s = lax.broadcasted_iota(jnp.int32, (t, plan.ts), 1).astype(F32)
    for sub in range(nsub):
        rows = pl.ds(sub * t, t)
        unperm = jnp.where(lane_s == slot_ref[rows, :1], 1.0, 0.0).astype(BF16)
        o_ref[rows, :] = x1_ref[rows, :] + g2_ref[...] * _dot(unperm, ybuf[sl * nsub + sub])


def _moe_tail(mixed, plan, m, row_fn, layer, wg, wu, wd):
    x1, hs, gs, slot, tab, wtab = mixed
    b, s, _ = x1.shape
    n = b * s
    arb = pltpu.CompilerParams(dimension_semantics=("arbitrary",), vmem_limit_bytes=VMEM_LIMIT)
    ys = pl.pallas_call(
        _moe_kernel,
        out_shape=jax.ShapeDtypeStruct((plan.rtot, D), BF16),
        grid_spec=pltpu.PrefetchScalarGridSpec(
            num_scalar_prefetch=1,
            grid=(plan.nw,),
            in_specs=[
                pl.BlockSpec((MOE_TM, D), lambda w, tb: (tb[0, w], 0)),
                pl.BlockSpec((MOE_TM, LANES), lambda w, tb: (tb[0, w], 0)),
                pl.BlockSpec((None, EPG, D, D_EXPERT), lambda w, tb: (layer, tb[2, w], 0, 0)),
                pl.BlockSpec((None, EPG, D, D_EXPERT), lambda w, tb: (layer, tb[2, w], 0, 0)),
                pl.BlockSpec((None, EPG, D_EXPERT, D), lambda w, tb: (layer, tb[2, w], 0, 0)),
            ],
            out_specs=pl.BlockSpec((MOE_TM, D), lambda w, tb: (tb[1, w], 0)),
            scratch_shapes=[pltpu.VMEM((EPG, D, D_EXPERT), BF16), pltpu.VMEM((EPG, D, D_EXPERT), BF16),
                            pltpu.VMEM((EPG, D_EXPERT, D), BF16)],
        ),
        compiler_params=arb,
        name="moe_experts",
    )(wtab, hs, gs, wg, wu, wd)

    step = next(r for r in (1024, 512, 256, plan.t) if s % r == 0 and r % plan.t == 0)
    nsub = step // plan.t
    out = pl.pallas_call(
        functools.partial(_combine_kernel, plan=plan, nsub=nsub),
        out_shape=jax.ShapeDtypeStruct((n, D), F32),
        grid_spec=pltpu.PrefetchScalarGridSpec(
            num_scalar_prefetch=1,
            grid=(n // step,),
            in_specs=[
                pl.BlockSpec((step, D), lambda i, tb: (i, 0)),
                pl.BlockSpec((step, LANES), lambda i, tb: (i, 0)),
                _mod_spec(5, lambda i, tb: row_fn(i * step // s)),
                pl.BlockSpec(memory_space=pl.ANY),
            ],
            out_specs=pl.BlockSpec((step, D), lambda i, tb: (i, 0)),
            scratch_shapes=[pltpu.VMEM((2 * nsub, plan.ts, D), BF16), pltpu.SemaphoreType.DMA((2 * nsub,))],
        ),
        compiler_params=arb,
        name="moe_combine",
    )(tab, x1.reshape(n, D), slot.reshape(n, LANES), m, ys)
    return out.reshape(b, s, D)


def _qkv_kernel(x_ref, sh1, sc1, n1g, w_ref, qg_ref, kg_ref, cos_ref, sin_ref, *out_refs, with_q):
    a = _rms_mod(x_ref[...], n1g[...], sh1[...], sc1[...]).astype(BF16)
    qkv = _dot(a, w_ref[...])
    cos, sin = cos_ref[...], sin_ref[...]
    lane = lax.broadcasted_iota(jnp.int32, cos.shape, 1)
    first_head = lane < HEAD_DIM
    first_half = (lane & 31) < 16

    def norm_rope(t, g):
        sq = t * t
        s_all = jnp.sum(sq, axis=-1, keepdims=True)
        s_lo = jnp.sum(jnp.where(first_head, sq, 0.0), axis=-1, keepdims=True)
        ms = jnp.where(first_head, s_lo, s_all - s_lo) * (1.0 / HEAD_DIM)
        tn = t * lax.rsqrt(ms + EPS) * g
        up = pltpu.roll(tn, LANES - 16, 1)
        dn = pltpu.roll(tn, 16, 1)
        return tn * cos + jnp.where(first_half, up, dn) * sin

    if with_q:
        q_ref, k_ref, v_ref = out_refs
        qg = qg_ref[...]
        for j in range(D // LANES):
            blk = norm_rope(qkv[:, j * LANES:(j + 1) * LANES], qg) * Q_SCALE
            q_ref[:, j * LANES:(j + 1) * LANES] = blk.astype(BF16)
        off = D
    else:
        k_ref, v_ref = out_refs
        off = 0
    kg = kg_ref[...]
    for j in range(KV_DIM // LANES):
        kt = norm_rope(qkv[:, off + j * LANES:off + (j + 1) * LANES], kg).T.astype(BF16)
        k_ref[2 * j] = kt[:HEAD_DIM]
        k_ref[2 * j + 1] = kt[HEAD_DIM:]
        vv = qkv[:, off + KV_DIM + j * LANES:off + KV_DIM + (j + 1) * LANES]
        ones_col = jnp.where(lane == HEAD_DIM, 1.0, 0.0)
        v_ref[2 * j] = jnp.where(first_head, vv, ones_col).astype(BF16)
        v_ref[2 * j + 1] = jnp.where(first_head, pltpu.roll(vv, HEAD_DIM, 1), ones_col).astype(BF16)


def _qkv_layer(x, m, row_fn, n1g, w, qg, kg, cos_t, sin_t, with_q):
    b, s, _ = x.shape
    ts = min(512, s)
    rf = lambda bi, i: row_fn(bi)
    shapes = [jax.ShapeDtypeStruct((b, N_KV, HEAD_DIM, s), BF16), jax.ShapeDtypeStruct((b, N_KV, s, LANES), BF16)]
    specs = [pl.BlockSpec((None, N_KV, HEAD_DIM, ts), lambda bi, i: (bi, 0, 0, i)),
             pl.BlockSpec((None, N_KV, ts, LANES), lambda bi, i: (bi, 0, i, 0))]
    if with_q:
        shapes = [jax.ShapeDtypeStruct((b, s, D), BF16)] + shapes
        specs = [pl.BlockSpec((None, ts, D), lambda bi, i: (bi, i, 0))] + specs
    return pl.pallas_call(
        functools.partial(_qkv_kernel, with_q=with_q),
        out_shape=tuple(shapes),
        grid=(b, s // ts),
        in_specs=[
            pl.BlockSpec((None, ts, D), lambda bi, i: (bi, i, 0)),
            _mod_spec(0, rf), _mod_spec(1, rf),
            _full_spec((1, D)),
            _full_spec(w.shape),
            _full_spec((1, LANES)), _full_spec((1, LANES)),
            pl.BlockSpec((ts, LANES), lambda bi, i: (i, 0)),
            pl.BlockSpec((ts, LANES), lambda bi, i: (i, 0)),
        ],
        out_specs=tuple(specs),
        compiler_params=pltpu.CompilerParams(dimension_semantics=("arbitrary", "arbitrary"),
                                             vmem_limit_bytes=VMEM_LIMIT),
        name="qkv_layer",
    )(x, m, m, n1g, w, qg, kg, cos_t, sin_t)


def _attn_kernel(q_ref, kt_ref, v_ref, o_ref, *, tq, tc):
    kt, v = kt_ref[...], v_ref[...]
    for r in range(tq // tc):
        for j in range(Q_PER_KV):
            rows, cols = pl.ds(r * tc, tc), pl.ds(j * HEAD_DIM, HEAD_DIM)
            s = _dot(q_ref[rows, cols], kt)
            m = jnp.max(s, axis=-1, keepdims=True)
            p = jnp.exp2(s - m)
            o = _dot(p.astype(BF16), v)
            o = o[:, :HEAD_DIM] / o[:, HEAD_DIM:HEAD_DIM + 1]
            o_ref[rows, cols] = o.astype(BF16)


def _attention(q, kt_all, v_all):
    b, s, _ = q.shape
    nk = v_all.shape[2]
    tq, tc = 512, 256
    grp = Q_PER_KV * HEAD_DIM
    return pl.pallas_call(
        functools.partial(_attn_kernel, tq=tq, tc=tc),
        out_shape=jax.ShapeDtypeStruct((b, s, D), BF16),
        grid=(b, N_KV, s // tq),
        in_specs=[
            pl.BlockSpec((None, tq, grp), lambda bi, h, i: (bi, i, h)),
            pl.BlockSpec((None, None, HEAD_DIM, nk), lambda bi, h, i: (bi, h, 0, 0)),
            pl.BlockSpec((None, None, nk, LANES), lambda bi, h, i: (bi, h, 0, 0)),
        ],
        out_specs=pl.BlockSpec((None, tq, grp), lambda bi, h, i: (bi, i, h)),
        compiler_params=pltpu.CompilerParams(dimension_semantics=("arbitrary", "arbitrary", "arbitrary"),
                                             vmem_limit_bytes=VMEM_LIMIT),
        name="attention",
    )(q, kt_all, v_all)


def _proj_kernel(o_ref, x_ref, g1, sh2, sc2, n2g, wo_ref, wrh_ref, wrl_ref, br_ref, x1_ref, *disp_refs, plan):
    d, step = _Dispatch(plan, disp_refs), _grid_step()
    _dispatch_begin(d, step)
    y = _dot(o_ref[...], wo_ref[...])
    _post_mixer(x_ref[...], y, g1[...], n2g[...], sh2[...], sc2[...], wrh_ref, wrl_ref, br_ref, x1_ref, d, step)


def _proj_layer(o, x, m, row_fn, n2g, wo, router):
    b, s, _ = x.shape
    ts = min(512, s)
    rf = lambda bi, i: row_fn(bi)
    plan = _make_plan(b, s, ts)
    shapes, specs, scratch = _mixer_out(b, s, ts, plan)
    return plan, pl.pallas_call(
        functools.partial(_proj_kernel, plan=plan),
        out_shape=shapes,
        scratch_shapes=scratch,
        grid=(b, s // ts),
        in_specs=[
            pl.BlockSpec((None, ts, D), lambda bi, i: (bi, i, 0)),
            pl.BlockSpec((None, ts, D), lambda bi, i: (bi, i, 0)),
            _mod_spec(2, rf), _mod_spec(3, rf), _mod_spec(4, rf),
            _full_spec((1, D)),
            _full_spec((D, D)),
        ] + _router_specs(),
        out_specs=specs,
        compiler_params=pltpu.CompilerParams(dimension_semantics=("arbitrary", "arbitrary"),
                                             vmem_limit_bytes=VMEM_LIMIT),
        name="attn_proj",
    )(o, x, m, m, m, n2g, wo, *router)


def _gelu_tanh(x):
    c = 0.7978845608028654
    return x * (0.5 * (1.0 + jnp.tanh(c * (x + 0.044715 * (x * x * x)))))


def _sgu_kernel(x_ref, sh1, sc1, g1, sh2, sc2, n1g, n2g, win_ref, bin_ref, lng_ref, lnb_ref, ws_ref,
                bs_ref, wout_ref, wrh_ref, wrl_ref, br_ref, x1_ref, *disp_refs, ts, plan):
    d, step = _Dispatch(plan, disp_refs), _grid_step()
    _dispatch_begin(d, step)
    x = x_ref[...]
    a = _rms_mod(x, n1g[...], sh1[...], sc1[...]).astype(BF16)
    z = _gelu_tanh(_dot(a, win_ref[...]) + bin_ref[...])
    u, v = z[:, :D], z[:, D:]
    mu = jnp.mean(v, axis=-1, keepdims=True)
    vc = v - mu
    var = jnp.mean(vc * vc, axis=-1, keepdims=True)
    vb = (vc * lax.rsqrt(var + EPS) * lng_ref[...] + lnb_ref[...]).astype(BF16)
    bias = bs_ref[...]
    rows = []
    for c in range(ts // CHUNK):
        cols = [_dot(ws_ref[g], vb[c * CHUNK:(c + 1) * CHUNK, g * LANES:(g + 1) * LANES])
                for g in range(SGU_G)]
        rows.append(jnp.concatenate(cols, axis=1) + bias)
    sv = jnp.concatenate(rows, axis=0)
    y = _dot((u * sv).astype(BF16), wout_ref[...])
    _post_mixer(x, y, g1[...], n2g[...], sh2[...], sc2[...], wrh_ref, wrl_ref, br_ref, x1_ref, d, step)


def _sgu_layer(x, m, row_fn, n1g, n2g, win, b_in, lng, lnb, ws, bs_full, wout, router):
    b, s, _ = x.shape
    ts = min(512, s)
    rf = lambda bi, i: row_fn(bi)
    plan = _make_plan(b, s, ts)
    shapes, specs, scratch = _mixer_out(b, s, ts, plan)
    return plan, pl.pallas_call(
        functools.partial(_sgu_kernel, ts=ts, plan=plan),
        out_shape=shapes,
        scratch_shapes=scratch,
        grid=(b, s // ts),
        in_specs=[
            pl.BlockSpec((None, ts, D), lambda bi, i: (bi, i, 0)),
            _mod_spec(0, rf), _mod_spec(1, rf), _mod_spec(2, rf), _mod_spec(3, rf), _mod_spec(4, rf),
            _full_spec((1, D)), _full_spec((1, D)),
            _full_spec((D, 2 * D)), _full_spec((1, 2 * D)),
            _full_spec((1, D)), _full_spec((1, D)),
            _full_spec((SGU_G, CHUNK, CHUNK)), _full_spec((CHUNK, D)),
            _full_spec((D, D)),
        ] + _router_specs(),
        out_specs=specs,
        compiler_params=pltpu.CompilerParams(dimension_semantics=("arbitrary", "arbitrary"),
                                             vmem_limit_bytes=VMEM_LIMIT),
        name="sgu_layer",
    )(x, m, m, m, m, m, n1g, n2g, win, b_in, lng, lnb, ws, bs_full, wout, *router)


def _rope_tables(s):
    t = jnp.arange(s)
    pos = jnp.stack([t // GRID_W, t % GRID_W], axis=-1).astype(F32)
    inv = ROPE_THETA ** (-jnp.arange(0, HEAD_DIM // 2, 2, dtype=F32) / (HEAD_DIM // 2))
    lane = jnp.arange(LANES)
    d = lane % HEAD_DIM
    ang = pos[:, d // 32] * inv[d % 16][None, :]
    sign = jnp.where((d % 32) < 16, -1.0, 1.0).astype(F32)
    return jnp.cos(ang), jnp.sin(ang) * sign[None, :]


def _router_params(w_rg, b_rg, w_re, b_re):
    w = jnp.concatenate([w_rg, w_re], axis=1)
    w = jnp.pad(w, ((0, 0), (0, LANES - w.shape[1])))
    hi = w.astype(BF16)
    lo = (w - hi.astype(F32)).astype(BF16)
    bias = jnp.concatenate([b_rg, b_re])
    bias = jnp.pad(bias, (0, LANES - bias.shape[0])).reshape(1, LANES)
    return hi, lo, bias


def kernel(x, c, ctx, c_ctx, w_mod, b_mod, norm1_g, norm2_g, pool_w, pool_b, pool_ls, attn_wqkv, attn_qg,
           attn_kg, attn_wo, sgu_win, sgu_bin, sgu_lng, sgu_lnb, sgu_ws, sgu_bs, sgu_wout, moe_wrg, moe_brg,
           moe_wre, moe_bre, moe_wg, moe_wu, moe_wd):
    b, s, _ = x.shape
    depth = w_mod.shape[0]
    assert b <= CTX_ROW and x.shape[2] == D and s % GRID_W == 0
    attn_layers = [i for i in range(depth) if i % N_MIX == 1]
    ctx_last = max(attn_layers, default=-1)

    cc = jnp.zeros((MOD_ROWS, D), F32).at[:b].set(c).at[CTX_ROW].set(c_ctx)
    mods = _adaln(cc, w_mod, b_mod).reshape(depth, MOD_ROWS, N_MOD, 1, D)
    lat_row = lambda bi: bi
    ctx_row = lambda bi: CTX_ROW
    cos_t, sin_t = _rope_tables(s)
    ones_t = jnp.ones((ctx.shape[1], LANES), F32)
    zeros_t = jnp.zeros((ctx.shape[1], LANES), F32)

    sctx = ctx
    for i in range(depth):
        kind, j = i % N_MIX, i // N_MIX
        ctx_in, ctx_upd = i <= ctx_last, i < ctx_last
        m = mods[i]
        n1g, n2g = norm1_g[i].reshape(1, D), norm2_g[i].reshape(1, D)
        router = _router_params(moe_wrg[i], moe_brg[i], moe_wre[i], moe_bre[i])
        streams = [(x, lat_row)] + ([(sctx, ctx_row)] if ctx_upd else [])
        outs = []
        if kind == 0:
            pw, pb = pool_w[j].astype(BF16), pool_b[j].reshape(1, D)
            pls = pool_ls[j].reshape(1, D)
            for xs, rf in streams:
                outs.append(_pool_layer(xs, m, rf, n1g, n2g, pw, pb, pls, router))
        elif kind == 1:
            w = attn_wqkv[j].astype(BF16)
            qg = jnp.tile(attn_qg[j], LANES // HEAD_DIM).reshape(1, LANES)
            kg = jnp.tile(attn_kg[j], LANES // HEAD_DIM).reshape(1, LANES)
            q, k, v = _qkv_layer(x, m, lat_row, n1g, w, qg, kg, cos_t, sin_t, True)
            k_c, v_c = _qkv_layer(sctx, m, ctx_row, n1g, w[:, D:], qg, kg, ones_t, zeros_t, False)
            o = _attention(q, jnp.concatenate([k_c, k], axis=3), jnp.concatenate([v_c, v], axis=2))
            outs.append(_proj_layer(o, x, m, lat_row, n2g, attn_wo[j].astype(BF16), router))
            assert not ctx_upd
        else:
            bs_full = jnp.repeat(sgu_bs[j].T, LANES, axis=1)
            args = (sgu_win[j].astype(BF16), sgu_bin[j].reshape(1, 2 * D), sgu_lng[j].reshape(1, D),
                    sgu_lnb[j].reshape(1, D), sgu_ws[j].astype(BF16), bs_full, sgu_wout[j].astype(BF16))
            for xs, rf in streams:
                outs.append(_sgu_layer(xs, m, rf, n1g, n2g, *args, router))
        new = [_moe_tail(mixed, plan, m, rf, i, moe_wg, moe_wu, moe_wd)
               for (plan, mixed), (_, rf) in zip(outs, streams)]
        x = new[0]
        if ctx_upd:
            sctx = new[1]
    return x
```

```python
import functools
from typing import NamedTuple

import jax
import jax.numpy as jnp
from jax import lax
from jax.experimental import pallas as pl
from jax.experimental.pallas import tpu as pltpu

F32 = jnp.float32
BF16 = jnp.bfloat16

D = 1024
GRID_W = 64
EPS = 1e-6
N_MIX = 3
POOL_WINDOWS = (2, 4, 8, 16)
POOL_CH = D // len(POOL_WINDOWS)
POOL_HALO = 8
HEAD_DIM = 64
N_KV = 4
Q_PER_KV = 4
KV_DIM = N_KV * HEAD_DIM
ROPE_THETA = 10000.0
Q_SCALE = HEAD_DIM ** -0.5 * 1.4426950408889634
CHUNK = 128
SGU_G = 8
N_GROUPS = 4
EPG = 4
D_EXPERT = 256
N_MOD = 6
LANES = 128
GROUP_LANE = 4
GATE_PIECE = 8
SORT_T = 256
RUN_ALIGN = 16
RUN_PAD = N_GROUPS * RUN_ALIGN
MOE_TM = 512
WTAB_ROWS = 5
CTX_ROW = 8
MOD_ROWS = 16
VMEM_LIMIT = 56 * 1024 * 1024


def _dot(a, b):
    return jnp.dot(a, b, preferred_element_type=F32)


def _rms_mod(x, g, sh, sc):
    ms = jnp.mean(x * x, axis=-1, keepdims=True)
    return x * lax.rsqrt(ms + EPS) * g * (1.0 + sc) + sh


def _silu(x):
    return x * (1.0 / (1.0 + jnp.exp(-x)))


def _route(h, wrh_ref, wrl_ref, br_ref):
    h_hi = h.astype(BF16)
    h_lo = (h - h_hi.astype(F32)).astype(BF16)
    whi = wrh_ref[...]
    lg = _dot(h_hi, whi) + _dot(h_hi, wrl_ref[...]) + _dot(h_lo, whi) + br_ref[...]
    lane = lax.broadcasted_iota(jnp.int32, lg.shape, 1)
    lane_f = lane.astype(F32)
    neg = -jnp.inf
    gmask = lane < N_GROUPS
    gl = jnp.where(gmask, lg, neg)
    gmax = jnp.max(gl, axis=-1, keepdims=True)
    gsel = jnp.min(jnp.where(gl == gmax, lane_f, float(LANES)), axis=-1, keepdims=True)
    gsum = jnp.sum(jnp.exp(gl - gmax), axis=-1, keepdims=True)
    g_w = 1.0 / gsum
    lo_lane = N_GROUPS + EPG * gsel
    emask = (lane_f >= lo_lane) & (lane_f < lo_lane + EPG)
    el = jnp.where(emask, lg, neg)
    t1 = jnp.max(el, axis=-1, keepdims=True)
    i1 = jnp.min(jnp.where(el == t1, lane_f, float(LANES)), axis=-1, keepdims=True)
    el2 = jnp.where(lane_f == i1, neg, el)
    t2 = jnp.max(el2, axis=-1, keepdims=True)
    i2 = jnp.min(jnp.where(el2 == t2, lane_f, float(LANES)), axis=-1, keepdims=True)
    e = jnp.exp(t2 - t1)
    w1 = 1.0 / (1.0 + e)
    w2 = e / (1.0 + e)
    rec = jnp.where(lane_f == i1 - lo_lane, w1 * g_w, jnp.where(lane_f == i2 - lo_lane, w2 * g_w, 0.0))
    return jnp.where(lane == GROUP_LANE, gsel, rec)


def _post_mixer(x, y, g1, n2g, sh2, sc2, wrh_ref, wrl_ref, br_ref, x1_ref, d, step):
    x1 = x + g1 * y
    h2 = _rms_mod(x1, n2g, sh2, sc2)
    x1_ref[...] = x1
    rec = _route(h2, wrh_ref, wrl_ref, br_ref)
    _dispatch_step(d, step, rec, h2.astype(BF16))


def _adaln_kernel(cc_ref, w_ref, b_ref, o_ref):
    s = _silu(cc_ref[...]).astype(BF16)
    o_ref[...] = _dot(s, w_ref[...].astype(BF16)) + b_ref[...]


def _adaln(cc, w_mod, b_mod):
    depth = w_mod.shape[0]
    return pl.pallas_call(
        _adaln_kernel,
        out_shape=jax.ShapeDtypeStruct((depth, MOD_ROWS, N_MOD * D), F32),
        grid=(depth, N_MOD),
        in_specs=[
            pl.BlockSpec((MOD_ROWS, D), lambda i, k: (0, 0)),
            pl.BlockSpec((None, D, D), lambda i, k: (i, 0, k)),
            pl.BlockSpec((None, 1, D), lambda i, k: (i, 0, k)),
        ],
        out_specs=pl.BlockSpec((None, MOD_ROWS, D), lambda i, k: (i, 0, k)),
        compiler_params=pltpu.CompilerParams(dimension_semantics=("arbitrary", "arbitrary")),
        name="adaln",
    )(cc, w_mod, b_mod.reshape(depth, 1, N_MOD * D))


def _mod_spec(k, row_fn):
    return pl.BlockSpec((None, None, 1, D), lambda *idx: (row_fn(*idx), k, 0, 0))


def _full_spec(shape):
    nd = len(shape)
    return pl.BlockSpec(shape, lambda *idx: (0,) * nd)


def _router_specs():
    return [_full_spec((D, LANES)), _full_spec((D, LANES)), _full_spec((1, LANES))]


class _Plan(NamedTuple):
    t: int
    ts: int
    nsub: int
    nsteps: int
    ntiles: int
    rcap: int
    rtot: int
    nw: int
    bits: tuple


def _make_plan(b, s, tile_rows):
    n = b * s
    t = min(SORT_T, tile_rows)
    assert n % MOE_TM == 0 and tile_rows % t == 0 and t % RUN_ALIGN == 0
    ntiles = n // t
    bits = tuple(t >> k for k in range(t.bit_length()) if (t >> k) >= RUN_ALIGN)
    return _Plan(t=t, ts=t + RUN_PAD, nsub=tile_rows // t, nsteps=n // tile_rows, ntiles=ntiles, rcap=n,
                 rtot=N_GROUPS * n + MOE_TM, nw=-(-ntiles * (t + RUN_PAD) // MOE_TM) + N_GROUPS, bits=bits)


class _Dispatch:
    def __init__(self, plan, refs):
        self.plan = plan
        (self.hs, self.gs, self.slot, self.tab, self.wtab,
         self.hbuf, self.gbuf, self.zh, self.zg, self.sems, self.cnt, self.pend,
         self.slot_sv, self.piece_sv, self.h_sv) = refs


def _mixer_out(b, s, ts, plan):
    tile = lambda bi, i: (bi, i, 0)
    any_spec = pl.BlockSpec(memory_space=pl.ANY)
    smem_spec = pl.BlockSpec(memory_space=pltpu.SMEM)
    shapes = (
        jax.ShapeDtypeStruct((b, s, D), F32),
        jax.ShapeDtypeStruct((plan.rtot, D), BF16),
        jax.ShapeDtypeStruct((plan.rtot, LANES), F32),
        jax.ShapeDtypeStruct((b, s, LANES), F32),
        jax.ShapeDtypeStruct((plan.ntiles, 2 * N_GROUPS), jnp.int32),
        jax.ShapeDtypeStruct((WTAB_ROWS, plan.nw), jnp.int32),
    )
    specs = (pl.BlockSpec((None, ts, D), tile), any_spec, any_spec, pl.BlockSpec((None, ts, LANES), tile),
             smem_spec, smem_spec)
    scratch = [
        pltpu.VMEM((2 * plan.nsub, plan.ts, D), BF16),
        pltpu.VMEM((2 * plan.nsub, plan.ts, LANES), F32),
        pltpu.VMEM((MOE_TM // 2, D), BF16),
        pltpu.VMEM((MOE_TM // 2, LANES), F32),
        pltpu.SemaphoreType.DMA((2 * plan.nsub,)),
        pltpu.SMEM((N_GROUPS,), jnp.int32),
        pltpu.SMEM((2 * plan.nsub, 3 * N_GROUPS), jnp.int32),
        pltpu.VMEM((2, ts, LANES), F32),
        pltpu.VMEM((2, ts, LANES), BF16),
        pltpu.VMEM((2, ts, D), BF16),
    ]
    return shapes, specs, scratch


def _grid_step():
    return pl.program_id(0) * pl.num_programs(1) + pl.program_id(1)


def _pool_kernel(xp_ref, xc_ref, xn_ref, sh1, sc1, g1, sh2, sc2, n1g, n2g, pw_ref, pb_ref, pls_ref,
                 wrh_ref, wrl_ref, br_ref, x1_ref, *disp_refs, seq, ts, plan):
    d, step = _Dispatch(plan, disp_refs), _grid_step()
    _dispatch_begin(d, step)
    i = pl.program_id(1)
    nt = pl.num_programs(1)
    x = xc_ref[...]
    g, sh, sc = n1g[...], sh1[...], sc1[...]
    a_c = _rms_mod(x, g, sh, sc)
    a_p = _rms_mod(xp_ref[...], g, sh, sc) * (i > 0).astype(F32)
    a_n = _rms_mod(xn_ref[...], g, sh, sc) * (i < nt - 1).astype(F32)
    a_ext = jnp.concatenate([a_p, a_c, a_n], axis=0)
    n_ext = ts + 2 * POOL_HALO
    t = i * ts + lax.broadcasted_iota(jnp.int32, (ts, 1), 0)
    ys = []
    for gi, w in enumerate(POOL_WINDOWS):
        ag = a_ext[:, gi * POOL_CH:(gi + 1) * POOL_CH]
        acc = ag + pltpu.roll(ag, 1, 0)
        half = 1
        while 2 * half < w:
            acc = pltpu.roll(acc, half, 0) + pltpu.roll(acc, n_ext - half, 0)
            half *= 2
        win = acc[POOL_HALO:POOL_HALO + ts]
        cnt = (jnp.minimum(t + w // 2, seq) - jnp.maximum(t - w // 2, 0)).astype(F32)
        p = win / cnt - ag[POOL_HALO:POOL_HALO + ts]
        ys.append(_dot(p.astype(BF16), pw_ref[gi]))
    y = (jnp.concatenate(ys, axis=1) + pb_ref[...]) * pls_ref[...]
    _post_mixer(x, y, g1[...], n2g[...], sh2[...], sc2[...], wrh_ref, wrl_ref, br_ref, x1_ref, d, step)


def _pool_layer(x, m, row_fn, n1g, n2g, pw, pb, pls, router):
    b, s, _ = x.shape
    ts = min(512, s)
    nh = s // POOL_HALO
    r = ts // POOL_HALO
    plan = _make_plan(b, s, ts)
    shapes, specs, scratch = _mixer_out(b, s, ts, plan)
    rf = lambda bi, i: row_fn(bi)
    return plan, pl.pallas_call(
        functools.partial(_pool_kernel, seq=s, ts=ts, plan=plan),
        out_shape=shapes,
        scratch_shapes=scratch,
        grid=(b, s // ts),
        in_specs=[
            pl.BlockSpec((None, POOL_HALO, D), lambda bi, i: (bi, jnp.maximum(i * r - 1, 0), 0)),
            pl.BlockSpec((None, ts, D), lambda bi, i: (bi, i, 0)),
            pl.BlockSpec((None, POOL_HALO, D), lambda bi, i: (bi, jnp.minimum((i + 1) * r, nh - 1), 0)),
            _mod_spec(0, rf), _mod_spec(1, rf), _mod_spec(2, rf), _mod_spec(3, rf), _mod_spec(4, rf),
            _full_spec((1, D)), _full_spec((1, D)),
            _full_spec(pw.shape), _full_spec((1, D)), _full_spec((1, D)),
        ] + _router_specs(),
        out_specs=specs,
        compiler_params=pltpu.CompilerParams(dimension_semantics=("arbitrary", "arbitrary"),
                                             vmem_limit_bytes=VMEM_LIMIT),
        name="pool_layer",
    )(x, x, x, m, m, m, m, m, n1g, n2g, pw, pb, pls, *router)


def _run_copies(src, dst, src_off, dst_off, n, sem, wait, bits, fixed_src=False):
    done = jnp.int32(0)
    for bit in bits:
        take = (n & bit) != 0

        @pl.when(take)
        def _(done=done, bit=bit):
            so = pl.multiple_of(src_off + (0 if fixed_src else done), RUN_ALIGN)
            do = pl.multiple_of(dst_off + done, RUN_ALIGN)
            cp = pltpu.make_async_copy(src.at[pl.ds(so, bit)], dst.at[pl.ds(do, bit)], sem)
            if wait:
                cp.wait()
            else:
                cp.start()

        done = done + jnp.where(take, bit, 0)


def _dispatch_runs(d, buf, wait):
    for g in range(N_GROUPS):
        off, n, dst = d.pend[buf, g], d.pend[buf, N_GROUPS + g], d.pend[buf, 2 * N_GROUPS + g]
        _run_copies(d.hbuf.at[buf], d.hs, off, dst, n, d.sems.at[buf], wait, d.plan.bits)
        _run_copies(d.gbuf.at[buf], d.gs, off, dst, n, d.sems.at[buf], wait, d.plan.bits)


def _dispatch_begin(d, step):
    nsub = d.plan.nsub

    @pl.when(step == 0)
    def _():
        for g in range(N_GROUPS):
            d.cnt[g] = 0
        for buf in range(2 * nsub):
            for k in range(3 * N_GROUPS):
                d.pend[buf, k] = 0
        d.zh[...] = jnp.zeros_like(d.zh)
        d.zg[...] = jnp.zeros_like(d.zg)
        d.slot_sv[...] = jnp.zeros_like(d.slot_sv)
        d.piece_sv[...] = jnp.zeros_like(d.piece_sv)
        d.h_sv[...] = jnp.zeros_like(d.h_sv)

    @pl.when(step >= 2)
    def _():
        for sub in range(nsub):
            _dispatch_runs(d, lax.rem(step, 2) * nsub + sub, True)


def _dispatch_consts(plan):
    t = plan.t
    lane = lax.broadcasted_iota(jnp.int32, (t, LANES), 1)
    before = lax.broadcasted_iota(jnp.int32, (t, t), 1) < lax.broadcasted_iota(jnp.int32, (t, t), 0)
    srow = lax.broadcasted_iota(jnp.int32, (plan.ts, t), 0).astype(F32)
    return lane, jnp.where(before, 1.0, 0.0).astype(BF16), srow


def _dispatch_plan(d, step, sub, rec, consts):
    plan = d.plan
    t, nsub = plan.t, plan.nsub
    lane, before, _ = consts
    buf, tile = lax.rem(step, 2) * nsub + sub, step * nsub + sub
    onehot = jnp.where(lane.astype(F32) == rec[:, GROUP_LANE:GROUP_LANE + 1], 1.0, 0.0)
    rank = _dot(before, onehot.astype(BF16))
    counts = jnp.sum(onehot, axis=0, keepdims=True)
    off = jnp.int32(0)
    off_lane = jnp.zeros((1, LANES), F32)
    for g in range(N_GROUPS):
        c = counts[0, g].astype(jnp.int32)
        cpad = ((c + (RUN_ALIGN - 1)) // RUN_ALIGN) * RUN_ALIGN
        d.pend[buf, g] = off
        d.pend[buf, N_GROUPS + g] = cpad
        d.pend[buf, 2 * N_GROUPS + g] = g * plan.rcap + d.cnt[g]
        d.tab[tile, g] = d.cnt[g]
        d.tab[tile, N_GROUPS + g] = cpad
        d.cnt[g] = d.cnt[g] + cpad
        off_lane = jnp.where(lane[:1] == g, off.astype(F32), off_lane)
        off = off + cpad
    slot = jnp.sum(onehot * (rank + off_lane), axis=-1, keepdims=True)
    slot_b = jnp.broadcast_to(slot, (t, LANES))
    r_hi = rec.astype(BF16).astype(F32)
    r_mid = (rec - r_hi).astype(BF16).astype(F32)
    r_lo = rec - r_hi - r_mid
    pieces = r_hi + pltpu.roll(r_mid, GATE_PIECE, 1) + pltpu.roll(r_lo, 2 * GATE_PIECE, 1)
    return slot_b, pieces.astype(BF16)


def _dispatch_move(d, buf, slot_b, pieces, h, consts):
    srow = consts[2]
    slot_row = slot_b.T[:1]
    perm = jnp.where(srow == slot_row, 1.0, 0.0).astype(BF16)
    d.hbuf[buf] = _dot(perm, h).astype(BF16)
    d.gbuf[buf] = _dot(perm, pieces)


def _dispatch_step(d, step, rec, hb):
    plan = d.plan
    t, nsub = plan.t, plan.nsub
    consts = _dispatch_consts(plan)
    cur = lax.rem(step, 2)
    prev = 1 - cur
    planned = []
    for sub in range(nsub):
        rows = slice(sub * t, (sub + 1) * t)
        slot_b, pieces = _dispatch_plan(d, step, sub, rec[rows], consts)
        d.slot[pl.ds(sub * t, t), :] = slot_b
        d.slot_sv[cur, pl.ds(sub * t, t), :] = slot_b
        d.piece_sv[cur, pl.ds(sub * t, t), :] = pieces
        planned.append((slot_b, pieces, hb[rows]))
    d.h_sv[cur] = hb
    for sub in range(nsub):
        rows = pl.ds(sub * t, t)
        _dispatch_move(d, prev * nsub + sub, d.slot_sv[prev, rows, :], d.piece_sv[prev, rows, :],
                       d.h_sv[prev, rows, :], consts)
    for sub in range(nsub):
        _dispatch_runs(d, prev * nsub + sub, False)

    @pl.when(step == plan.nsteps - 1)
    def _():
        for sub, (slot_b, pieces, h) in enumerate(planned):
            _dispatch_move(d, cur * nsub + sub, slot_b, pieces, h, consts)
        for sub in range(nsub):
            _dispatch_runs(d, cur * nsub + sub, False)
        for buf in range(2 * nsub):
            _dispatch_runs(d, buf, True)
        blocks_per_region = plan.rcap // MOE_TM
        tail_bits = tuple(b for b in (MOE_TM >> k for k in range(1, MOE_TM.bit_length())) if b >= RUN_ALIGN)
        starts, total = [], jnp.int32(0)
        for wait in (False, True):
            for g in range(N_GROUPS):
                c = d.cnt[g]
                nt = (c + (MOE_TM - 1)) // MOE_TM
                for zbuf, dst in ((d.zh, d.hs), (d.zg, d.gs)):
                    _run_copies(zbuf, dst, 0, g * plan.rcap + c, nt * MOE_TM - c, d.sems.at[0], wait, tail_bits,
                                fixed_src=True)
                if not wait:
                    starts.append(total)
                    total = total + nt

        def item(w, carry):
            wc = jnp.minimum(w, total - 1)
            grp = sum((wc >= starts[g]).astype(jnp.int32) for g in range(1, N_GROUPS))
            first = jnp.int32(0)
            for g in range(1, N_GROUPS):
                first = jnp.where(grp == g, starts[g], first)
            blk = grp * blocks_per_region + (wc - first)
            valid = (w < total).astype(jnp.int32)
            d.wtab[0, w] = blk
            d.wtab[1, w] = jnp.where(valid == 1, blk, N_GROUPS * blocks_per_region)
            d.wtab[2, w] = grp
            d.wtab[3, w] = valid
            d.wtab[4, w] = valid * (wc == first).astype(jnp.int32)
            return carry

        lax.fori_loop(0, plan.nw, item, 0)


def _moe_kernel(wtab, hs_ref, gs_ref, wg32_ref, wu32_ref, wd32_ref, ys_ref, wg_ref, wu_ref, wd_ref):
    w = pl.program_id(0)

    @pl.when(wtab[4, w] == 1)
    def _():
        wg_ref[...] = wg32_ref[...].astype(BF16)
        wu_ref[...] = wu32_ref[...].astype(BF16)
        wd_ref[...] = wd32_ref[...].astype(BF16)

    @pl.when(wtab[3, w] == 1)
    def _():
        h = hs_ref[...]
        gates = gs_ref[...]
        parts = []
        for e in range(EPG):
            a = _dot(h, wg_ref[e])
            u = _dot(h, wu_ref[e])
            gate = sum(gates[:, k * GATE_PIECE + e:k * GATE_PIECE + e + 1] for k in range(3))
            parts.append((_silu(a) * u * gate).astype(BF16))
        hid = jnp.concatenate(parts, axis=1)
        ys_ref[...] = _dot(hid, wd_ref[...].reshape(EPG * D_EXPERT, D)).astype(BF16)

    @pl.when(wtab[3, w] == 0)
    def _():
        ys_ref[...] = jnp.zeros_like(ys_ref)


def _combine_kernel(tab, x1_ref, slot_ref, g2_ref, ys_ref, o_ref, ybuf, sems, *, plan, nsub):
    i = pl.program_id(0)
    nsteps = plan.ntiles // nsub
    sl = lax.rem(i, 2)
    t = plan.t

    def gather(step, parity, wait):
        for sub in range(nsub):
            tile, buf = step * nsub + sub, parity * nsub + sub
            off = jnp.int32(0)
            for g in range(N_GROUPS):
                n = tab[tile, N_GROUPS + g]
                _run_copies(ys_ref, ybuf.at[buf], g * plan.rcap + tab[tile, g], off, n, sems.at[buf], wait, plan.bits)
                off = off + n

    @pl.when(i == 0)
    def _():
        ybuf[...] = jnp.zeros_like(ybuf)
        gather(0, 0, False)

    @pl.when(i + 1 < nsteps)
    def _():
        gather(i + 1, 1 - sl, False)

    gather(i, sl, True)
    lane_s = lax.broadcasted_iota(jnp.int32, (t, plan.ts), 1).astype(F32)
    for sub in range(nsub):
        rows = pl.ds(sub * t, t)
        unperm = jnp.where(lane_s == slot_ref[rows, :1], 1.0, 0.0).astype(BF16)
        o_ref[rows, :] = x1_ref[rows, :] + g2_ref[...] * _dot(unperm, ybuf[sl * nsub + sub])


def _moe_tail(mixed, plan, m, row_fn, layer, wg, wu, wd):
    x1, hs, gs, slot, tab, wtab = mixed
    b, s, _ = x1.shape
    n = b * s
    arb = pltpu.CompilerParams(dimension_semantics=("arbitrary",), vmem_limit_bytes=VMEM_LIMIT)
    ys = pl.pallas_call(
        _moe_kernel,
        out_shape=jax.ShapeDtypeStruct((plan.rtot, D), BF16),
        grid_spec=pltpu.PrefetchScalarGridSpec(
            num_scalar_prefetch=1,
            grid=(plan.nw,),
            in_specs=[
                pl.BlockSpec((MOE_TM, D), lambda w, tb: (tb[0, w], 0)),
                pl.BlockSpec((MOE_TM, LANES), lambda w, tb: (tb[0, w], 0)),
                pl.BlockSpec((None, EPG, D, D_EXPERT), lambda w, tb: (layer, tb[2, w], 0, 0)),
                pl.BlockSpec((None, EPG, D, D_EXPERT), lambda w, tb: (layer, tb[2, w], 0, 0)),
                pl.BlockSpec((None, EPG, D_EXPERT, D), lambda w, tb: (layer, tb[2, w], 0, 0)),
            ],
            out_specs=pl.BlockSpec((MOE_TM, D), lambda w, tb: (tb[1, w], 0)),
            scratch_shapes=[pltpu.VMEM((EPG, D, D_EXPERT), BF16), pltpu.VMEM((EPG, D, D_EXPERT), BF16),
                            pltpu.VMEM((EPG, D_EXPERT, D), BF16)],
        ),
        compiler_params=arb,
        name="moe_experts",
    )(wtab, hs, gs, wg, wu, wd)

    step = next(r for r in (1024, 512, 256, plan.t) if s % r == 0 and r % plan.t == 0)
    nsub = step // plan.t
    out = pl.pallas_call(
        functools.partial(_combine_kernel, plan=plan, nsub=nsub),
        out_shape=jax.ShapeDtypeStruct((n, D), F32),
        grid_spec=pltpu.PrefetchScalarGridSpec(
            num_scalar_prefetch=1,
            grid=(n // step,),
            in_specs=[
                pl.BlockSpec((step, D), lambda i, tb: (i, 0)),
                pl.BlockSpec((step, LANES), lambda i, tb: (i, 0)),
                _mod_spec(5, lambda i, tb: row_fn(i * step // s)),
                pl.BlockSpec(memory_space=pl.ANY),
            ],
            out_specs=pl.BlockSpec((step, D), lambda i, tb: (i, 0)),
            scratch_shapes=[pltpu.VMEM((2 * nsub, plan.ts, D), BF16), pltpu.SemaphoreType.DMA((2 * nsub,))],
        ),
        compiler_params=arb,
        name="moe_combine",
    )(tab, x1.reshape(n, D), slot.reshape(n, LANES), m, ys)
    return out.reshape(b, s, D)


def _qkv_kernel(x_ref, sh1, sc1, n1g, w_ref, qg_ref, kg_ref, cos_ref, sin_ref, *out_refs, with_q):
    a = _rms_mod(x_ref[...], n1g[...], sh1[...], sc1[...]).astype(BF16)
    qkv = _dot(a, w_ref[...])
    cos, sin = cos_ref[...], sin_ref[...]
    lane = lax.broadcasted_iota(jnp.int32, cos.shape, 1)
    first_head = lane < HEAD_DIM
    first_half = (lane & 31) < 16

    def norm_rope(t, g):
        sq = t * t
        s_all = jnp.sum(sq, axis=-1, keepdims=True)
        s_lo = jnp.sum(jnp.where(first_head, sq, 0.0), axis=-1, keepdims=True)
        ms = jnp.where(first_head, s_lo, s_all - s_lo) * (1.0 / HEAD_DIM)
        tn = t * lax.rsqrt(ms + EPS) * g
        up = pltpu.roll(tn, LANES - 16, 1)
        dn = pltpu.roll(tn, 16, 1)
        return tn * cos + jnp.where(first_half, up, dn) * sin

    if with_q:
        q_ref, k_ref, v_ref = out_refs
        qg = qg_ref[...]
        for j in range(D // LANES):
            blk = norm_rope(qkv[:, j * LANES:(j + 1) * LANES], qg) * Q_SCALE
            q_ref[:, j * LANES:(j + 1) * LANES] = blk.astype(BF16)
        off = D
    else:
        k_ref, v_ref = out_refs
        off = 0
    kg = kg_ref[...]
    for j in range(KV_DIM // LANES):
        kt = norm_rope(qkv[:, off + j * LANES:off + (j + 1) * LANES], kg).T.astype(BF16)
        k_ref[2 * j] = kt[:HEAD_DIM]
        k_ref[2 * j + 1] = kt[HEAD_DIM:]
        vv = qkv[:, off + KV_DIM + j * LANES:off + KV_DIM + (j + 1) * LANES]
        ones_col = jnp.where(lane == HEAD_DIM, 1.0, 0.0)
        v_ref[2 * j] = jnp.where(first_head, vv, ones_col).astype(BF16)
        v_ref[2 * j + 1] = jnp.where(first_head, pltpu.roll(vv, HEAD_DIM, 1), ones_col).astype(BF16)


def _qkv_layer(x, m, row_fn, n1g, w, qg, kg, cos_t, sin_t, with_q):
    b, s, _ = x.shape
    ts = min(512, s)
    rf = lambda bi, i: row_fn(bi)
    shapes = [jax.ShapeDtypeStruct((b, N_KV, HEAD_DIM, s), BF16), jax.ShapeDtypeStruct((b, N_KV, s, LANES), BF16)]
    specs = [pl.BlockSpec((None, N_KV, HEAD_DIM, ts), lambda bi, i: (bi, 0, 0, i)),
             pl.BlockSpec((None, N_KV, ts, LANES), lambda bi, i: (bi, 0, i, 0))]
    if with_q:
        shapes = [jax.ShapeDtypeStruct((b, s, D), BF16)] + shapes
        specs = [pl.BlockSpec((None, ts, D), lambda bi, i: (bi, i, 0))] + specs
    return pl.pallas_call(
        functools.partial(_qkv_kernel, with_q=with_q),
        out_shape=tuple(shapes),
        grid=(b, s // ts),
        in_specs=[
            pl.BlockSpec((None, ts, D), lambda bi, i: (bi, i, 0)),
            _mod_spec(0, rf), _mod_spec(1, rf),
            _full_spec((1, D)),
            _full_spec(w.shape),
            _full_spec((1, LANES)), _full_spec((1, LANES)),
            pl.BlockSpec((ts, LANES), lambda bi, i: (i, 0)),
            pl.BlockSpec((ts, LANES), lambda bi, i: (i, 0)),
        ],
        out_specs=tuple(specs),
        compiler_params=pltpu.CompilerParams(dimension_semantics=("arbitrary", "arbitrary"),
                                             vmem_limit_bytes=VMEM_LIMIT),
        name="qkv_layer",
    )(x, m, m, n1g, w, qg, kg, cos_t, sin_t)


def _attn_kernel(q_ref, kt_ref, v_ref, o_ref, *, tq, tc):
    kt, v = kt_ref[...], v_ref[...]
    for r in range(tq // tc):
        for j in range(Q_PER_KV):
            rows, cols = pl.ds(r * tc, tc), pl.ds(j * HEAD_DIM, HEAD_DIM)
            s = _dot(q_ref[rows, cols], kt)
            m = jnp.max(s, axis=-1, keepdims=True)
            p = jnp.exp2(s - m)
            o = _dot(p.astype(BF16), v)
            o = o[:, :HEAD_DIM] / o[:, HEAD_DIM:HEAD_DIM + 1]
            o_ref[rows, cols] = o.astype(BF16)


def _attention(q, kt_all, v_all):
    b, s, _ = q.shape
    nk = v_all.shape[2]
    tq, tc = 512, 256
    grp = Q_PER_KV * HEAD_DIM
    return pl.pallas_call(
        functools.partial(_attn_kernel, tq=tq, tc=tc),
        out_shape=jax.ShapeDtypeStruct((b, s, D), BF16),
        grid=(b, N_KV, s // tq),
        in_specs=[
            pl.BlockSpec((None, tq, grp), lambda bi, h, i: (bi, i, h)),
            pl.BlockSpec((None, None, HEAD_DIM, nk), lambda bi, h, i: (bi, h, 0, 0)),
            pl.BlockSpec((None, None, nk, LANES), lambda bi, h, i: (bi, h, 0, 0)),
        ],
        out_specs=pl.BlockSpec((None, tq, grp), lambda bi, h, i: (bi, i, h)),
        compiler_params=pltpu.CompilerParams(dimension_semantics=("arbitrary", "arbitrary", "arbitrary"),
                                             vmem_limit_bytes=VMEM_LIMIT),
        name="attention",
    )(q, kt_all, v_all)


def _proj_kernel(o_ref, x_ref, g1, sh2, sc2, n2g, wo_ref, wrh_ref, wrl_ref, br_ref, x1_ref, *disp_refs, plan):
    d, step = _Dispatch(plan, disp_refs), _grid_step()
    _dispatch_begin(d, step)
    y = _dot(o_ref[...], wo_ref[...])
    _post_mixer(x_ref[...], y, g1[...], n2g[...], sh2[...], sc2[...], wrh_ref, wrl_ref, br_ref, x1_ref, d, step)


def _proj_layer(o, x, m, row_fn, n2g, wo, router):
    b, s, _ = x.shape
    ts = min(512, s)
    rf = lambda bi, i: row_fn(bi)
    plan = _make_plan(b, s, ts)
    shapes, specs, scratch = _mixer_out(b, s, ts, plan)
    return plan, pl.pallas_call(
        functools.partial(_proj_kernel, plan=plan),
        out_shape=shapes,
        scratch_shapes=scratch,
        grid=(b, s // ts),
        in_specs=[
            pl.BlockSpec((None, ts, D), lambda bi, i: (bi, i, 0)),
            pl.BlockSpec((None, ts, D), lambda bi, i: (bi, i, 0)),
            _mod_spec(2, rf), _mod_spec(3, rf), _mod_spec(4, rf),
            _full_spec((1, D)),
            _full_spec((D, D)),
        ] + _router_specs(),
        out_specs=specs,
        compiler_params=pltpu.CompilerParams(dimension_semantics=("arbitrary", "arbitrary"),
                                             vmem_limit_bytes=VMEM_LIMIT),
        name="attn_proj",
    )(o, x, m, m, m, n2g, wo, *router)


def _gelu_tanh(x):
    c = 0.7978845608028654
    return x * (0.5 * (1.0 + jnp.tanh(c * (x + 0.044715 * (x * x * x)))))


def _sgu_kernel(x_ref, sh1, sc1, g1, sh2, sc2, n1g, n2g, win_ref, bin_ref, lng_ref, lnb_ref, ws_ref,
                bs_ref, wout_ref, wrh_ref, wrl_ref, br_ref, x1_ref, *disp_refs, ts, plan):
    d, step = _Dispatch(plan, disp_refs), _grid_step()
    _dispatch_begin(d, step)
    x = x_ref[...]
    a = _rms_mod(x, n1g[...], sh1[...], sc1[...]).astype(BF16)
    z = _gelu_tanh(_dot(a, win_ref[...]) + bin_ref[...])
    u, v = z[:, :D], z[:, D:]
    mu = jnp.mean(v, axis=-1, keepdims=True)
    vc = v - mu
    var = jnp.mean(vc * vc, axis=-1, keepdims=True)
    vb = (vc * lax.rsqrt(var + EPS) * lng_ref[...] + lnb_ref[...]).astype(BF16)
    bias = bs_ref[...]
    rows = []
    for c in range(ts // CHUNK):
        cols = [_dot(ws_ref[g], vb[c * CHUNK:(c + 1) * CHUNK, g * LANES:(g + 1) * LANES])
                for g in range(SGU_G)]
        rows.append(jnp.concatenate(cols, axis=1) + bias)
    sv = jnp.concatenate(rows, axis=0)
    y = _dot((u * sv).astype(BF16), wout_ref[...])
    _post_mixer(x, y, g1[...], n2g[...], sh2[...], sc2[...], wrh_ref, wrl_ref, br_ref, x1_ref, d, step)


def _sgu_layer(x, m, row_fn, n1g, n2g, win, b_in, lng, lnb, ws, bs_full, wout, router):
    b, s, _ = x.shape
    ts = min(512, s)
    rf = lambda bi, i: row_fn(bi)
    plan = _make_plan(b, s, ts)
    shapes, specs, scratch = _mixer_out(b, s, ts, plan)
    return plan, pl.pallas_call(
        functools.partial(_sgu_kernel, ts=ts, plan=plan),
        out_shape=shapes,
        scratch_shapes=scratch,
        grid=(b, s // ts),
        in_specs=[
            pl.BlockSpec((None, ts, D), lambda bi, i: (bi, i, 0)),
            _mod_spec(0, rf), _mod_spec(1, rf), _mod_spec(2, rf), _mod_spec(3, rf), _mod_spec(4, rf),
            _full_spec((1, D)), _full_spec((1, D)),
            _full_spec((D, 2 * D)), _full_spec((1, 2 * D)),
            _full_spec((1, D)), _full_spec((1, D)),
            _full_spec((SGU_G, CHUNK, CHUNK)), _full_spec((CHUNK, D)),
            _full_spec((D, D)),
        ] + _router_specs(),
        out_specs=specs,
        compiler_params=pltpu.CompilerParams(dimension_semantics=("arbitrary", "arbitrary"),
                                             vmem_limit_bytes=VMEM_LIMIT),
        name="sgu_layer",
    )(x, m, m, m, m, m, n1g, n2g, win, b_in, lng, lnb, ws, bs_full, wout, *router)


def _rope_tables(s):
    t = jnp.arange(s)
    pos = jnp.stack([t // GRID_W, t % GRID_W], axis=-1).astype(F32)
    inv = ROPE_THETA ** (-jnp.arange(0, HEAD_DIM // 2, 2, dtype=F32) / (HEAD_DIM // 2))
    lane = jnp.arange(LANES)
    d = lane % HEAD_DIM
    ang = pos[:, d // 32] * inv[d % 16][None, :]
    sign = jnp.where((d % 32) < 16, -1.0, 1.0).astype(F32)
    return jnp.cos(ang), jnp.sin(ang) * sign[None, :]


def _router_params(w_rg, b_rg, w_re, b_re):
    w = jnp.concatenate([w_rg, w_re], axis=1)
    w = jnp.pad(w, ((0, 0), (0, LANES - w.shape[1])))
    hi = w.astype(BF16)
    lo = (w - hi.astype(F32)).astype(BF16)
    bias = jnp.concatenate([b_rg, b_re])
    bias = jnp.pad(bias, (0, LANES - bias.shape[0])).reshape(1, LANES)
    return hi, lo, bias


def kernel(x, c, ctx, c_ctx, w_mod, b_mod, norm1_g, norm2_g, pool_w, pool_b, pool_ls, attn_wqkv, attn_qg,
           attn_kg, attn_wo, sgu_win, sgu_bin, sgu_lng, sgu_lnb, sgu_ws, sgu_bs, sgu_wout, moe_wrg, moe_brg,
           moe_wre, moe_bre, moe_wg, moe_wu, moe_wd):
    b, s, _ = x.shape
    depth = w_mod.shape[0]
    assert b <= CTX_ROW and x.shape[2] == D and s % GRID_W == 0
    attn_layers = [i for i in range(depth) if i % N_MIX == 1]
    ctx_last = max(attn_layers, default=-1)

    cc = jnp.zeros((MOD_ROWS, D), F32).at[:b].set(c).at[CTX_ROW].set(c_ctx)
    mods = _adaln(cc, w_mod, b_mod).reshape(depth, MOD_ROWS, N_MOD, 1, D)
    lat_row = lambda bi: bi
    ctx_row = lambda bi: CTX_ROW
    cos_t, sin_t = _rope_tables(s)
    ones_t = jnp.ones((ctx.shape[1], LANES), F32)
    zeros_t = jnp.zeros((ctx.shape[1], LANES), F32)

    sctx = ctx
    for i in range(depth):
        kind, j = i % N_MIX, i // N_MIX
        ctx_in, ctx_upd = i <= ctx_last, i < ctx_last
        m = mods[i]
        n1g, n2g = norm1_g[i].reshape(1, D), norm2_g[i].reshape(1, D)
        router = _router_params(moe_wrg[i], moe_brg[i], moe_wre[i], moe_bre[i])
        streams = [(x, lat_row)] + ([(sctx, ctx_row)] if ctx_upd else [])
        outs = []
        if kind == 0:
            pw, pb = pool_w[j].astype(BF16), pool_b[j].reshape(1, D)
            pls = pool_ls[j].reshape(1, D)
            for xs, rf in streams:
                outs.append(_pool_layer(xs, m, rf, n1g, n2g, pw, pb, pls, router))
        elif kind == 1:
            w = attn_wqkv[j].astype(BF16)
            qg = jnp.tile(attn_qg[j], LANES // HEAD_DIM).reshape(1, LANES)
            kg = jnp.tile(attn_kg[j], LANES // HEAD_DIM).reshape(1, LANES)
            q, k, v = _qkv_layer(x, m, lat_row, n1g, w, qg, kg, cos_t, sin_t, True)
            k_c, v_c = _qkv_layer(sctx, m, ctx_row, n1g, w[:, D:], qg, kg, ones_t, zeros_t, False)
            o = _attention(q, jnp.concatenate([k_c, k], axis=3), jnp.concatenate([v_c, v], axis=2))
            outs.append(_proj_layer(o, x, m, lat_row, n2g, attn_wo[j].astype(BF16), router))
            assert not ctx_upd
        else:
            bs_full = jnp.repeat(sgu_bs[j].T, LANES, axis=1)
            args = (sgu_win[j].astype(BF16), sgu_bin[j].reshape(1, 2 * D), sgu_lng[j].reshape(1, D),
                    sgu_lnb[j].reshape(1, D), sgu_ws[j].astype(BF16), bs_full, sgu_wout[j].astype(BF16))
            for xs, rf in streams:
                outs.append(_sgu_layer(xs, m, rf, n1g, n2g, *args, router))
        new = [_moe_tail(mixed, plan, m, rf, i, moe_wg, moe_wu, moe_wd)
               for (plan, mixed), (_, rf) in zip(outs, streams)]
        x = new[0]
        if ctx_upd:
            sctx = new[1]
    return x
```

```python
import functools

import jax
import jax.numpy as jnp
from jax import lax
from jax.experimental import pallas as pl
from jax.experimental.pallas import tpu as pltpu

F32 = jnp.float32
BF16 = jnp.bfloat16

D = 1024
GRID_W = 64
EPS = 1e-6
N_MIX = 3
POOL_WINDOWS = (2, 4, 8, 16)
POOL_CH = D // len(POOL_WINDOWS)
POOL_HALO = 8
HEAD_DIM = 64
N_KV = 4
Q_PER_KV = 4
KV_DIM = N_KV * HEAD_DIM
ROPE_THETA = 10000.0
Q_SCALE = HEAD_DIM ** -0.5 * 1.4426950408889634
CHUNK = 128
SGU_G = 8
N_GROUPS = 4
EPG = 4
D_EXPERT = 256
N_MOD = 6
LANES = 128
GROUP_LANE = 4
GATE_PIECE = 8
SORT_T = 512
RUN_ALIGN = 16
SORT_TS = SORT_T + N_GROUPS * RUN_ALIGN
RUN_BITS = tuple(SORT_T >> k for k in range(6))
MOE_TM = 512
WTAB_ROWS = 5
ROW_W = D + LANES
CTX_ROW = 8
MOD_ROWS = 16
VMEM_LIMIT = 56 * 1024 * 1024


def _dot(a, b):
    return jnp.dot(a, b, preferred_element_type=F32)


def _rms_mod(x, g, sh, sc):
    ms = jnp.mean(x * x, axis=-1, keepdims=True)
    return x * lax.rsqrt(ms + EPS) * g * (1.0 + sc) + sh


def _silu(x):
    return x * (1.0 / (1.0 + jnp.exp(-x)))


def _route(h, wrh_ref, wrl_ref, br_ref):
    h_hi = h.astype(BF16)
    h_lo = (h - h_hi.astype(F32)).astype(BF16)
    whi = wrh_ref[...]
    lg = _dot(h_hi, whi) + _dot(h_hi, wrl_ref[...]) + _dot(h_lo, whi) + br_ref[...]
    lane = lax.broadcasted_iota(jnp.int32, lg.shape, 1)
    lane_f = lane.astype(F32)
    neg = -jnp.inf
    gmask = lane < N_GROUPS
    gl = jnp.where(gmask, lg, neg)
    gmax = jnp.max(gl, axis=-1, keepdims=True)
    gsel = jnp.min(jnp.where(gl == gmax, lane_f, float(LANES)), axis=-1, keepdims=True)
    gsum = jnp.sum(jnp.exp(gl - gmax), axis=-1, keepdims=True)
    g_w = 1.0 / gsum
    lo_lane = N_GROUPS + EPG * gsel
    emask = (lane_f >= lo_lane) & (lane_f < lo_lane + EPG)
    el = jnp.where(emask, lg, neg)
    t1 = jnp.max(el, axis=-1, keepdims=True)
    i1 = jnp.min(jnp.where(el == t1, lane_f, float(LANES)), axis=-1, keepdims=True)
    el2 = jnp.where(lane_f == i1, neg, el)
    t2 = jnp.max(el2, axis=-1, keepdims=True)
    i2 = jnp.min(jnp.where(el2 == t2, lane_f, float(LANES)), axis=-1, keepdims=True)
    e = jnp.exp(t2 - t1)
    w1 = 1.0 / (1.0 + e)
    w2 = e / (1.0 + e)
    rec = jnp.where(lane_f == i1 - lo_lane, w1 * g_w, jnp.where(lane_f == i2 - lo_lane, w2 * g_w, 0.0))
    return jnp.where(lane == GROUP_LANE, gsel, rec)


def _post_mixer(x, y, g1, n2g, sh2, sc2, wrh_ref, wrl_ref, br_ref, x1_ref, h2_ref, gates_ref):
    x1 = x + g1 * y
    h2 = _rms_mod(x1, n2g, sh2, sc2)
    x1_ref[...] = x1
    h2_ref[...] = h2.astype(BF16)
    gates_ref[...] = _route(h2, wrh_ref, wrl_ref, br_ref)


def _adaln_kernel(cc_ref, w_ref, b_ref, o_ref):
    s = _silu(cc_ref[...]).astype(BF16)
    o_ref[...] = _dot(s, w_ref[...].astype(BF16)) + b_ref[...]


def _adaln(cc, w_mod, b_mod):
    depth = w_mod.shape[0]
    return pl.pallas_call(
        _adaln_kernel,
        out_shape=jax.ShapeDtypeStruct((depth, MOD_ROWS, N_MOD * D), F32),
        grid=(depth, N_MOD),
        in_specs=[
            pl.BlockSpec((MOD_ROWS, D), lambda i, k: (0, 0)),
            pl.BlockSpec((None, D, D), lambda i, k: (i, 0, k)),
            pl.BlockSpec((None, 1, D), lambda i, k: (i, 0, k)),
        ],
        out_specs=pl.BlockSpec((None, MOD_ROWS, D), lambda i, k: (i, 0, k)),
        compiler_params=pltpu.CompilerParams(dimension_semantics=("arbitrary", "arbitrary")),
        name="adaln",
    )(cc, w_mod, b_mod.reshape(depth, 1, N_MOD * D))


def _mod_spec(k, row_fn):
    return pl.BlockSpec((None, None, 1, D), lambda *idx: (row_fn(*idx), k, 0, 0))


def _full_spec(shape):
    nd = len(shape)
    return pl.BlockSpec(shape, lambda *idx: (0,) * nd)


def _router_specs():
    return [_full_spec((D, LANES)), _full_spec((D, LANES)), _full_spec((1, LANES))]


def _post_out(rows_shape, ts):
    b, s = rows_shape
    shapes = (
        jax.ShapeDtypeStruct((b, s, D), F32),
        jax.ShapeDtypeStruct((b, s, D), BF16),
        jax.ShapeDtypeStruct((b, s, LANES), F32),
    )
    specs = (
        pl.BlockSpec((None, ts, D), lambda bi, i: (bi, i, 0)),
        pl.BlockSpec((None, ts, D), lambda bi, i: (bi, i, 0)),
        pl.BlockSpec((None, ts, LANES), lambda bi, i: (bi, i, 0)),
    )
    return shapes, specs


def _pool_kernel(xp_ref, xc_ref, xn_ref, sh1, sc1, g1, sh2, sc2, n1g, n2g, pw_ref, pb_ref, pls_ref,
                 wrh_ref, wrl_ref, br_ref, x1_ref, h2_ref, gates_ref, *, seq, ts):
    i = pl.program_id(1)
    nt = pl.num_programs(1)
    x = xc_ref[...]
    g, sh, sc = n1g[...], sh1[...], sc1[...]
    a_c = _rms_mod(x, g, sh, sc)
    a_p = _rms_mod(xp_ref[...], g, sh, sc) * (i > 0).astype(F32)
    a_n = _rms_mod(xn_ref[...], g, sh, sc) * (i < nt - 1).astype(F32)
    a_ext = jnp.concatenate([a_p, a_c, a_n], axis=0)
    n_ext = ts + 2 * POOL_HALO
    t = i * ts + lax.broadcasted_iota(jnp.int32, (ts, 1), 0)
    ys = []
    for gi, w in enumerate(POOL_WINDOWS):
        ag = a_ext[:, gi * POOL_CH:(gi + 1) * POOL_CH]
        acc = ag + pltpu.roll(ag, 1, 0)
        half = 1
        while 2 * half < w:
            acc = pltpu.roll(acc, half, 0) + pltpu.roll(acc, n_ext - half, 0)
            half *= 2
        win = acc[POOL_HALO:POOL_HALO + ts]
        cnt = (jnp.minimum(t + w // 2, seq) - jnp.maximum(t - w // 2, 0)).astype(F32)
        p = win / cnt - ag[POOL_HALO:POOL_HALO + ts]
        ys.append(_dot(p.astype(BF16), pw_ref[gi]))
    y = (jnp.concatenate(ys, axis=1) + pb_ref[...]) * pls_ref[...]
    _post_mixer(x, y, g1[...], n2g[...], sh2[...], sc2[...], wrh_ref, wrl_ref, br_ref,
                x1_ref, h2_ref, gates_ref)


def _pool_layer(x, m, row_fn, n1g, n2g, pw, pb, pls, router):
    b, s, _ = x.shape
    ts = min(512, s)
    nh = s // POOL_HALO
    r = ts // POOL_HALO
    shapes, specs = _post_out((b, s), ts)
    rf = lambda bi, i: row_fn(bi)
    return pl.pallas_call(
        functools.partial(_pool_kernel, seq=s, ts=ts),
        out_shape=shapes,
        grid=(b, s // ts),
        in_specs=[
            pl.BlockSpec((None, POOL_HALO, D), lambda bi, i: (bi, jnp.maximum(i * r - 1, 0), 0)),
            pl.BlockSpec((None, ts, D), lambda bi, i: (bi, i, 0)),
            pl.BlockSpec((None, POOL_HALO, D), lambda bi, i: (bi, jnp.minimum((i + 1) * r, nh - 1), 0)),
            _mod_spec(0, rf), _mod_spec(1, rf), _mod_spec(2, rf), _mod_spec(3, rf), _mod_spec(4, rf),
            _full_spec((1, D)), _full_spec((1, D)),
            _full_spec(pw.shape), _full_spec((1, D)), _full_spec((1, D)),
        ] + _router_specs(),
        out_specs=specs,
        compiler_params=pltpu.CompilerParams(dimension_semantics=("arbitrary", "arbitrary"),
                                             vmem_limit_bytes=VMEM_LIMIT),
        name="pool_layer",
    )(x, x, x, m, m, m, m, m, n1g, n2g, pw, pb, pls, *router)


def _run_copies(src, dst, src_off, dst_off, n, sem, wait, fixed_src=False):
    done = jnp.int32(0)
    for bit in RUN_BITS:
        take = (n & bit) != 0

        @pl.when(take)
        def _(done=done, bit=bit):
            so = pl.multiple_of(src_off + (0 if fixed_src else done), RUN_ALIGN)
            do = pl.multiple_of(dst_off + done, RUN_ALIGN)
            cp = pltpu.make_async_copy(src.at[pl.ds(so, bit)], dst.at[pl.ds(do, bit)], sem)
            if wait:
                cp.wait()
            else:
                cp.start()

        done = done + jnp.where(take, bit, 0)


def _wait_rows(src, dst, n, sem, base):
    pltpu.make_async_copy(src.at[pl.ds(0, base)], dst.at[pl.ds(0, base)], sem).wait()
    _run_copies(src, dst, 0, 0, n - base, sem, True, fixed_src=True)


def _dispatch_kernel(h_ref, r_ref, hx_ref, slot_ref, tab_ref, wtab_ref,
                     hbuf, zh, sems, cnt, pend, *, ntiles, nsub, rcap, nw):
    i = pl.program_id(0)
    nsteps = ntiles // nsub
    sl = lax.rem(i, 2)
    t = SORT_T

    @pl.when(i == 0)
    def _():
        for g in range(N_GROUPS):
            cnt[g] = 0
        zh[...] = jnp.zeros_like(zh)

    def runs(slot_idx, wait):
        if wait:
            total = sum(pend[slot_idx, N_GROUPS + g] for g in range(N_GROUPS))
            _wait_rows(hbuf.at[slot_idx], hx_ref, total, sems.at[slot_idx], t)
            return
        for g in range(N_GROUPS):
            off, n, dst = pend[slot_idx, g], pend[slot_idx, N_GROUPS + g], pend[slot_idx, 2 * N_GROUPS + g]
            _run_copies(hbuf.at[slot_idx], hx_ref, off, dst, n, sems.at[slot_idx], False)

    @pl.when(i >= 2)
    def _():
        for sub in range(nsub):
            runs(sl * nsub + sub, True)

    lane = lax.broadcasted_iota(jnp.int32, (t, LANES), 1)
    lane_f = lane.astype(F32)
    before = lax.broadcasted_iota(jnp.int32, (t, t), 1) < lax.broadcasted_iota(jnp.int32, (t, t), 0)
    before = jnp.where(before, 1.0, 0.0).astype(BF16)
    srow = lax.broadcasted_iota(jnp.int32, (SORT_TS, t), 0).astype(F32)
    for sub in range(nsub):
        bi, tile, rows = sl * nsub + sub, i * nsub + sub, pl.ds(sub * t, t)
        r = r_ref[rows, :]
        onehot = jnp.where(lane_f == r[:, GROUP_LANE:GROUP_LANE + 1], 1.0, 0.0)
        rank = _dot(before, onehot.astype(BF16))
        counts = jnp.sum(onehot, axis=0, keepdims=True)
        off = jnp.int32(0)
        off_lane = jnp.zeros((1, LANES), F32)
        for g in range(N_GROUPS):
            c = counts[0, g].astype(jnp.int32)
            cpad = ((c + (RUN_ALIGN - 1)) // RUN_ALIGN) * RUN_ALIGN
            pend[bi, g] = off
            pend[bi, N_GROUPS + g] = cpad
            pend[bi, 2 * N_GROUPS + g] = g * rcap + cnt[g]
            tab_ref[tile, g] = cnt[g]
            tab_ref[tile, N_GROUPS + g] = cpad
            cnt[g] = cnt[g] + cpad
            off_lane = jnp.where(lane[:1] == g, off.astype(F32), off_lane)
            off = off + cpad
        slot = jnp.sum(onehot * (rank + off_lane), axis=-1, keepdims=True)
        slot_b = jnp.broadcast_to(slot, (t, LANES))
        slot_ref[rows, :] = slot_b
        slot_row = slot_b.T[:1]
        perm = jnp.where(srow == slot_row, 1.0, 0.0).astype(BF16)
        hbuf[bi, :, :D] = _dot(perm, h_ref[rows, :]).astype(BF16)
        r_hi = r.astype(BF16).astype(F32)
        r_mid = (r - r_hi).astype(BF16).astype(F32)
        r_lo = r - r_hi - r_mid
        pieces = r_hi + pltpu.roll(r_mid, GATE_PIECE, 1) + pltpu.roll(r_lo, 2 * GATE_PIECE, 1)
        hbuf[bi, :, D:] = _dot(perm, pieces.astype(BF16)).astype(BF16)
    for sub in range(nsub):
        runs(sl * nsub + sub, False)

    @pl.when(i == nsteps - 1)
    def _():
        if nsteps >= 2:
            for sub in range(nsub):
                runs((1 - sl) * nsub + sub, True)
        for sub in range(nsub):
            runs(sl * nsub + sub, True)
        blocks_per_region = rcap // MOE_TM
        starts, total = [], jnp.int32(0)
        for wait in (False, True):
            for g in range(N_GROUPS):
                c = cnt[g]
                nt = (c + (MOE_TM - 1)) // MOE_TM
                _run_copies(zh, hx_ref, 0, g * rcap + c, nt * MOE_TM - c, sems.at[0], wait, fixed_src=True)
                if not wait:
                    starts.append(total)
                    total = total + nt

        def item(w, carry):
            wc = jnp.minimum(w, total - 1)
            grp = sum((wc >= starts[g]).astype(jnp.int32) for g in range(1, N_GROUPS))
            first = jnp.int32(0)
            for g in range(1, N_GROUPS):
                first = jnp.where(grp == g, starts[g], first)
            blk = grp * blocks_per_region + (wc - first)
            valid = (w < total).astype(jnp.int32)
            wtab_ref[0, w] = blk
            wtab_ref[1, w] = jnp.where(valid == 1, blk, N_GROUPS * blocks_per_region)
            wtab_ref[2, w] = grp
            wtab_ref[3, w] = valid
            wtab_ref[4, w] = valid * (wc == first).astype(jnp.int32)
            return carry

        lax.fori_loop(0, nw, item, 0)


def _moe_kernel(wtab, hx_ref, wg32_ref, wu32_ref, wd32_ref, ys_ref, wg_ref, wu_ref, wd_ref):
    w = pl.program_id(0)

    @pl.when(wtab[4, w] == 1)
    def _():
        wg_ref[...] = wg32_ref[...].astype(BF16)
        wu_ref[...] = wu32_ref[...].astype(BF16)
        wd_ref[...] = wd32_ref[...].astype(BF16)

    @pl.when(wtab[3, w] == 1)
    def _():
        h = hx_ref[:, :D]
        gates = hx_ref[:, D:].astype(F32)
        parts = []
        for e in range(EPG):
            a = _dot(h, wg_ref[e])
            u = _dot(h, wu_ref[e])
            gate = sum(gates[:, k * GATE_PIECE + e:k * GATE_PIECE + e + 1] for k in range(3))
            parts.append((_silu(a) * u * gate).astype(BF16))
        hid = jnp.concatenate(parts, axis=1)
        ys_ref[...] = _dot(hid, wd_ref[...].reshape(EPG * D_EXPERT, D)).astype(BF16)

    @pl.when(wtab[3, w] == 0)
    def _():
        ys_ref[...] = jnp.zeros_like(ys_ref)


def _combine_kernel(tab, x1_ref, slot_ref, g2_ref, ys_ref, o_ref, ybuf, sems, *, ntiles, nsub, rcap):
    i = pl.program_id(0)
    nsteps = ntiles // nsub
    sl = lax.rem(i, 2)

    def gather(step, parity, wait):
        for sub in range(nsub):
            tile, bi = step * nsub + sub, parity * nsub + sub
            if wait:
                total = sum(tab[tile, N_GROUPS + g] for g in range(N_GROUPS))
                _wait_rows(ys_ref, ybuf.at[bi], total, sems.at[bi], SORT_T)
                continue
            off = jnp.int32(0)
            for g in range(N_GROUPS):
                n = tab[tile, N_GROUPS + g]
                _run_copies(ys_ref, ybuf.at[bi], g * rcap + tab[tile, g], off, n, sems.at[bi], False)
                off = off + n

    @pl.when(i == 0)
    def _():
        ybuf[...] = jnp.zeros_like(ybuf)
        gather(0, 0, False)

    @pl.when(i + 1 < nsteps)
    def _():
        gather(i + 1, 1 - sl, False)

    gather(i, sl, True)
    lane_s = lax.broadcasted_iota(jnp.int32, (SORT_T, SORT_TS), 1).astype(F32)
    for sub in range(nsub):
        rows = pl.ds(sub * SORT_T, SORT_T)
        unperm = jnp.where(lane_s == slot_ref[rows, :1], 1.0, 0.0).astype(BF16)
        o_ref[rows, :] = x1_ref[rows, :] + g2_ref[...] * _dot(unperm, ybuf[sl * nsub + sub])


def _moe_layer(h2, x1, route, m, row_fn, layer, wg, wu, wd):
    b, s, _ = x1.shape
    n = b * s
    assert n % SORT_T == 0 and n % MOE_TM == 0
    ntiles = n // SORT_T
    nsub = 2 if ntiles % 2 == 0 and s % (2 * SORT_T) == 0 else 1
    step = nsub * SORT_T
    rcap = n
    rtot = N_GROUPS * rcap + MOE_TM
    nw = -(-ntiles * SORT_TS // MOE_TM) + N_GROUPS
    arb = pltpu.CompilerParams(dimension_semantics=("arbitrary",), vmem_limit_bytes=VMEM_LIMIT)
    any_spec = pl.BlockSpec(memory_space=pl.ANY)
    smem_spec = pl.BlockSpec(memory_space=pltpu.SMEM)
    hx, slot, tab, wtab = pl.pallas_call(
        functools.partial(_dispatch_kernel, ntiles=ntiles, nsub=nsub, rcap=rcap, nw=nw),
        out_shape=(
            jax.ShapeDtypeStruct((rtot, ROW_W), BF16),
            jax.ShapeDtypeStruct((n, LANES), F32),
            jax.ShapeDtypeStruct((ntiles, 2 * N_GROUPS), jnp.int32),
            jax.ShapeDtypeStruct((WTAB_ROWS, nw), jnp.int32),
        ),
        grid=(ntiles // nsub,),
        in_specs=[pl.BlockSpec((step, D), lambda i: (i, 0)), pl.BlockSpec((step, LANES), lambda i: (i, 0))],
        out_specs=(any_spec, pl.BlockSpec((step, LANES), lambda i: (i, 0)), smem_spec, smem_spec),
        scratch_shapes=[
            pltpu.VMEM((2 * nsub, SORT_TS, ROW_W), BF16),
            pltpu.VMEM((MOE_TM // 2, ROW_W), BF16),
            pltpu.SemaphoreType.DMA((2 * nsub,)),
            pltpu.SMEM((N_GROUPS,), jnp.int32),
            pltpu.SMEM((2 * nsub, 3 * N_GROUPS), jnp.int32),
        ],
        compiler_params=arb,
        name="moe_dispatch",
    )(h2.reshape(n, D), route.reshape(n, LANES))

    ys = pl.pallas_call(
        _moe_kernel,
        out_shape=jax.ShapeDtypeStruct((rtot, D), BF16),
        grid_spec=pltpu.PrefetchScalarGridSpec(
            num_scalar_prefetch=1,
            grid=(nw,),
            in_specs=[
                pl.BlockSpec((MOE_TM, ROW_W), lambda w, tb: (tb[0, w], 0)),
                pl.BlockSpec((None, EPG, D, D_EXPERT), lambda w, tb: (layer, tb[2, w], 0, 0)),
                pl.BlockSpec((None, EPG, D, D_EXPERT), lambda w, tb: (layer, tb[2, w], 0, 0)),
                pl.BlockSpec((None, EPG, D_EXPERT, D), lambda w, tb: (layer, tb[2, w], 0, 0)),
            ],
            out_specs=pl.BlockSpec((MOE_TM, D), lambda w, tb: (tb[1, w], 0)),
            scratch_shapes=[pltpu.VMEM((EPG, D, D_EXPERT), BF16), pltpu.VMEM((EPG, D, D_EXPERT), BF16),
                            pltpu.VMEM((EPG, D_EXPERT, D), BF16)],
        ),
        compiler_params=arb,
        name="moe_experts",
    )(wtab, hx, wg, wu, wd)

    out = pl.pallas_call(
        functools.partial(_combine_kernel, ntiles=ntiles, nsub=nsub, rcap=rcap),
        out_shape=jax.ShapeDtypeStruct((n, D), F32),
        grid_spec=pltpu.PrefetchScalarGridSpec(
            num_scalar_prefetch=1,
            grid=(ntiles // nsub,),
            in_specs=[
                pl.BlockSpec((step, D), lambda i, tb: (i, 0)),
                pl.BlockSpec((step, LANES), lambda i, tb: (i, 0)),
                _mod_spec(5, lambda i, tb: row_fn(i * step // s)),
                any_spec,
            ],
            out_specs=pl.BlockSpec((step, D), lambda i, tb: (i, 0)),
            scratch_shapes=[pltpu.VMEM((2 * nsub, SORT_TS, D), BF16), pltpu.SemaphoreType.DMA((2 * nsub,))],
        ),
        compiler_params=arb,
        name="moe_combine",
    )(tab, x1.reshape(n, D), slot, m, ys)
    return out.reshape(b, s, D)


def _qkv_kernel(x_ref, sh1, sc1, n1g, w_ref, qg_ref, kg_ref, cos_ref, sin_ref, *out_refs, with_q):
    a = _rms_mod(x_ref[...], n1g[...], sh1[...], sc1[...]).astype(BF16)
    qkv = _dot(a, w_ref[...])
    cos, sin = cos_ref[...], sin_ref[...]
    lane = lax.broadcasted_iota(jnp.int32, cos.shape, 1)
    first_head = lane < HEAD_DIM
    first_half = (lane & 31) < 16

    def norm_rope(t, g):
        sq = t * t
        s_all = jnp.sum(sq, axis=-1, keepdims=True)
        s_lo = jnp.sum(jnp.where(first_head, sq, 0.0), axis=-1, keepdims=True)
        ms = jnp.where(first_head, s_lo, s_all - s_lo) * (1.0 / HEAD_DIM)
        tn = t * lax.rsqrt(ms + EPS) * g
        up = pltpu.roll(tn, LANES - 16, 1)
        dn = pltpu.roll(tn, 16, 1)
        return tn * cos + jnp.where(first_half, up, dn) * sin

    if with_q:
        q_ref, k_ref, v_ref = out_refs
        qg = qg_ref[...]
        for j in range(D // LANES):
            blk = norm_rope(qkv[:, j * LANES:(j + 1) * LANES], qg) * Q_SCALE
            q_ref[:, j * LANES:(j + 1) * LANES] = blk.astype(BF16)
        off = D
    else:
        k_ref, v_ref = out_refs
        off = 0
    kg = kg_ref[...]
    for j in range(KV_DIM // LANES):
        kt = norm_rope(qkv[:, off + j * LANES:off + (j + 1) * LANES], kg).T.astype(BF16)
        k_ref[2 * j] = kt[:HEAD_DIM]
        k_ref[2 * j + 1] = kt[HEAD_DIM:]
        vv = qkv[:, off + KV_DIM + j * LANES:off + KV_DIM + (j + 1) * LANES]
        ones_col = jnp.where(lane == HEAD_DIM, 1.0, 0.0)
        v_ref[2 * j] = jnp.where(first_head, vv, ones_col).astype(BF16)
        v_ref[2 * j + 1] = jnp.where(first_head, pltpu.roll(vv, HEAD_DIM, 1), ones_col).astype(BF16)


def _qkv_layer(x, m, row_fn, n1g, w, qg, kg, cos_t, sin_t, with_q):
    b, s, _ = x.shape
    ts = min(512, s)
    rf = lambda bi, i: row_fn(bi)
    shapes = [jax.ShapeDtypeStruct((b, N_KV, HEAD_DIM, s), BF16), jax.ShapeDtypeStruct((b, N_KV, s, LANES), BF16)]
    specs = [pl.BlockSpec((None, N_KV, HEAD_DIM, ts), lambda bi, i: (bi, 0, 0, i)),
             pl.BlockSpec((None, N_KV, ts, LANES), lambda bi, i: (bi, 0, i, 0))]
    if with_q:
        shapes = [jax.ShapeDtypeStruct((b, s, D), BF16)] + shapes
        specs = [pl.BlockSpec((None, ts, D), lambda bi, i: (bi, i, 0))] + specs
    return pl.pallas_call(
        functools.partial(_qkv_kernel, with_q=with_q),
        out_shape=tuple(shapes),
        grid=(b, s // ts),
        in_specs=[
            pl.BlockSpec((None, ts, D), lambda bi, i: (bi, i, 0)),
            _mod_spec(0, rf), _mod_spec(1, rf),
            _full_spec((1, D)),
            _full_spec(w.shape),
            _full_spec((1, LANES)), _full_spec((1, LANES)),
            pl.BlockSpec((ts, LANES), lambda bi, i: (i, 0)),
            pl.BlockSpec((ts, LANES), lambda bi, i: (i, 0)),
        ],
        out_specs=tuple(specs),
        compiler_params=pltpu.CompilerParams(dimension_semantics=("arbitrary", "arbitrary"),
                                             vmem_limit_bytes=VMEM_LIMIT),
        name="qkv_layer",
    )(x, m, m, n1g, w, qg, kg, cos_t, sin_t)


def _attn_kernel(q_ref, kt_ref, v_ref, o_ref, *, tq, tc):
    kt, v = kt_ref[...], v_ref[...]
    for r in range(tq // tc):
        for j in range(Q_PER_KV):
            rows, cols = pl.ds(r * tc, tc), pl.ds(j * HEAD_DIM, HEAD_DIM)
            s = _dot(q_ref[rows, cols], kt)
            m = jnp.max(s, axis=-1, keepdims=True)
            p = jnp.exp2(s - m)
            o = _dot(p.astype(BF16), v)
            o = o[:, :HEAD_DIM] / o[:, HEAD_DIM:HEAD_DIM + 1]
            o_ref[rows, cols] = o.astype(BF16)


def _attention(q, kt_all, v_all):
    b, s, _ = q.shape
    nk = v_all.shape[2]
    tq, tc = 512, 256
    grp = Q_PER_KV * HEAD_DIM
    return pl.pallas_call(
        functools.partial(_attn_kernel, tq=tq, tc=tc),
        out_shape=jax.ShapeDtypeStruct((b, s, D), BF16),
        grid=(b, N_KV, s // tq),
        in_specs=[
            pl.BlockSpec((None, tq, grp), lambda bi, h, i: (bi, i, h)),
            pl.BlockSpec((None, None, HEAD_DIM, nk), lambda bi, h, i: (bi, h, 0, 0)),
            pl.BlockSpec((None, None, nk, LANES), lambda bi, h, i: (bi, h, 0, 0)),
        ],
        out_specs=pl.BlockSpec((None, tq, grp), lambda bi, h, i: (bi, i, h)),
        compiler_params=pltpu.CompilerParams(dimension_semantics=("arbitrary", "arbitrary", "arbitrary"),
                                             vmem_limit_bytes=VMEM_LIMIT),
        name="attention",
    )(q, kt_all, v_all)


def _proj_kernel(o_ref, x_ref, g1, sh2, sc2, n2g, wo_ref, wrh_ref, wrl_ref, br_ref,
                 x1_ref, h2_ref, gates_ref):
    y = _dot(o_ref[...], wo_ref[...])
    _post_mixer(x_ref[...], y, g1[...], n2g[...], sh2[...], sc2[...], wrh_ref, wrl_ref, br_ref,
                x1_ref, h2_ref, gates_ref)


def _proj_layer(o, x, m, row_fn, n2g, wo, router):
    b, s, _ = x.shape
    ts = min(512, s)
    rf = lambda bi, i: row_fn(bi)
    shapes, specs = _post_out((b, s), ts)
    return pl.pallas_call(
        _proj_kernel,
        out_shape=shapes,
        grid=(b, s // ts),
        in_specs=[
            pl.BlockSpec((None, ts, D), lambda bi, i: (bi, i, 0)),
            pl.BlockSpec((None, ts, D), lambda bi, i: (bi, i, 0)),
            _mod_spec(2, rf), _mod_spec(3, rf), _mod_spec(4, rf),
            _full_spec((1, D)),
            _full_spec((D, D)),
        ] + _router_specs(),
        out_specs=specs,
        compiler_params=pltpu.CompilerParams(dimension_semantics=("arbitrary", "arbitrary"),
                                             vmem_limit_bytes=VMEM_LIMIT),
        name="attn_proj",
    )(o, x, m, m, m, n2g, wo, *router)


def _gelu_tanh(x):
    c = 0.7978845608028654
    return x * (0.5 * (1.0 + jnp.tanh(c * (x + 0.044715 * (x * x * x)))))


def _sgu_kernel(x_ref, sh1, sc1, g1, sh2, sc2, n1g, n2g, win_ref, bin_ref, lng_ref, lnb_ref, ws_ref,
                bs_ref, wout_ref, wrh_ref, wrl_ref, br_ref, x1_ref, h2_ref, gates_ref, *, ts):
    x = x_ref[...]
    a = _rms_mod(x, n1g[...], sh1[...], sc1[...]).astype(BF16)
    z = _gelu_tanh(_dot(a, win_ref[...]) + bin_ref[...])
    u, v = z[:, :D], z[:, D:]
    mu = jnp.mean(v, axis=-1, keepdims=True)
    vc = v - mu
    var = jnp.mean(vc * vc, axis=-1, keepdims=True)
    vb = (vc * lax.rsqrt(var + EPS) * lng_ref[...] + lnb_ref[...]).astype(BF16)
    bias = bs_ref[...]
    rows = []
    for c in range(ts // CHUNK):
        cols = [_dot(ws_ref[g], vb[c * CHUNK:(c + 1) * CHUNK, g * LANES:(g + 1) * LANES])
                for g in range(SGU_G)]
        rows.append(jnp.concatenate(cols, axis=1) + bias)
    sv = jnp.concatenate(rows, axis=0)
    y = _dot((u * sv).astype(BF16), wout_ref[...])
    _post_mixer(x, y, g1[...], n2g[...], sh2[...], sc2[...], wrh_ref, wrl_ref, br_ref,
                x1_ref, h2_ref, gates_ref)


def _sgu_layer(x, m, row_fn, n1g, n2g, win, b_in, lng, lnb, ws, bs_full, wout, router):
    b, s, _ = x.shape
    ts = min(512, s)
    rf = lambda bi, i: row_fn(bi)
    shapes, specs = _post_out((b, s), ts)
    return pl.pallas_call(
        functools.partial(_sgu_kernel, ts=ts),
        out_shape=shapes,
        grid=(b, s // ts),
        in_specs=[
            pl.BlockSpec((None, ts, D), lambda bi, i: (bi, i, 0)),
            _mod_spec(0, rf), _mod_spec(1, rf), _mod_spec(2, rf), _mod_spec(3, rf), _mod_spec(4, rf),
            _full_spec((1, D)), _full_spec((1, D)),
            _full_spec((D, 2 * D)), _full_spec((1, 2 * D)),
            _full_spec((1, D)), _full_spec((1, D)),
            _full_spec((SGU_G, CHUNK, CHUNK)), _full_spec((CHUNK, D)),
            _full_spec((D, D)),
        ] + _router_specs(),
        out_specs=specs,
        compiler_params=pltpu.CompilerParams(dimension_semantics=("arbitrary", "arbitrary"),
                                             vmem_limit_bytes=VMEM_LIMIT),
        name="sgu_layer",
    )(x, m, m, m, m, m, n1g, n2g, win, b_in, lng, lnb, ws, bs_full, wout, *router)


def _rope_tables(s):
    t = jnp.arange(s)
    pos = jnp.stack([t // GRID_W, t % GRID_W], axis=-1).astype(F32)
    inv = ROPE_THETA ** (-jnp.arange(0, HEAD_DIM // 2, 2, dtype=F32) / (HEAD_DIM // 2))
    lane = jnp.arange(LANES)
    d = lane % HEAD_DIM
    ang = pos[:, d // 32] * inv[d % 16][None, :]
    sign = jnp.where((d % 32) < 16, -1.0, 1.0).astype(F32)
    return jnp.cos(ang), jnp.sin(ang) * sign[None, :]


def _router_params(w_rg, b_rg, w_re, b_re):
    w = jnp.concatenate([w_rg, w_re], axis=1)
    w = jnp.pad(w, ((0, 0), (0, LANES - w.shape[1])))
    hi = w.astype(BF16)
    lo = (w - hi.astype(F32)).astype(BF16)
    bias = jnp.concatenate([b_rg, b_re])
    bias = jnp.pad(bias, (0, LANES - bias.shape[0])).reshape(1, LANES)
    return hi, lo, bias


def kernel(x, c, ctx, c_ctx, w_mod, b_mod, norm1_g, norm2_g, pool_w, pool_b, pool_ls, attn_wqkv, attn_qg,
           attn_kg, attn_wo, sgu_win, sgu_bin, sgu_lng, sgu_lnb, sgu_ws, sgu_bs, sgu_wout, moe_wrg, moe_brg,
           moe_wre, moe_bre, moe_wg, moe_wu, moe_wd):
    b, s, _ = x.shape
    depth = w_mod.shape[0]
    assert b <= CTX_ROW and x.shape[2] == D and s % GRID_W == 0
    attn_layers = [i for i in range(depth) if i % N_MIX == 1]
    ctx_last = max(attn_layers, default=-1)

    cc = jnp.zeros((MOD_ROWS, D), F32).at[:b].set(c).at[CTX_ROW].set(c_ctx)
    mods = _adaln(cc, w_mod, b_mod).reshape(depth, MOD_ROWS, N_MOD, 1, D)
    lat_row = lambda bi: bi
    ctx_row = lambda bi: CTX_ROW
    cos_t, sin_t = _rope_tables(s)
    ones_t = jnp.ones((ctx.shape[1], LANES), F32)
    zeros_t = jnp.zeros((ctx.shape[1], LANES), F32)

    sctx = ctx
    for i in range(depth):
        kind, j = i % N_MIX, i // N_MIX
        ctx_in, ctx_upd = i <= ctx_last, i < ctx_last
        m = mods[i]
        n1g, n2g = norm1_g[i].reshape(1, D), norm2_g[i].reshape(1, D)
        router = _router_params(moe_wrg[i], moe_brg[i], moe_wre[i], moe_bre[i])
        streams = [(x, lat_row)] + ([(sctx, ctx_row)] if ctx_upd else [])
        outs = []
        if kind == 0:
            pw, pb = pool_w[j].astype(BF16), pool_b[j].reshape(1, D)
            pls = pool_ls[j].reshape(1, D)
            for xs, rf in streams:
                outs.append(_pool_layer(xs, m, rf, n1g, n2g, pw, pb, pls, router))
        elif kind == 1:
            w = attn_wqkv[j].astype(BF16)
            qg = jnp.tile(attn_qg[j], LANES // HEAD_DIM).reshape(1, LANES)
            kg = jnp.tile(attn_kg[j], LANES // HEAD_DIM).reshape(1, LANES)
            q, k, v = _qkv_layer(x, m, lat_row, n1g, w, qg, kg, cos_t, sin_t, True)
            k_c, v_c = _qkv_layer(sctx, m, ctx_row, n1g, w[:, D:], qg, kg, ones_t, zeros_t, False)
            o = _attention(q, jnp.concatenate([k_c, k], axis=3), jnp.concatenate([v_c, v], axis=2))
            outs.append(_proj_layer(o, x, m, lat_row, n2g, attn_wo[j].astype(BF16), router))
            assert not ctx_upd
        else:
            bs_full = jnp.repeat(sgu_bs[j].T, LANES, axis=1)
            args = (sgu_win[j].astype(BF16), sgu_bin[j].reshape(1, 2 * D), sgu_lng[j].reshape(1, D),
                    sgu_lnb[j].reshape(1, D), sgu_ws[j].astype(BF16), bs_full, sgu_wout[j].astype(BF16))
            for xs, rf in streams:
                outs.append(_sgu_layer(xs, m, rf, n1g, n2g, *args, router))
        new = [_moe_layer(h2, x1, gates, m, rf, i, moe_wg, moe_wu, moe_wd)
               for (x1, h2, gates), (_, rf) in zip(outs, streams)]
        x = new[0]
        if ctx_upd:
            sctx = new[1]
    return x
```
